```python
import math
import jax, jax.numpy as jnp
from jax import lax
import numpy as np


D_MODEL = 2048
BATCH = 8
SEQ = 2048
DEPTH = 2

N_MIXERS = 2
N_CONV_LAYERS = (DEPTH + 1) // 2
N_ATTN_LAYERS = DEPTH // 2
N_HEADS = 16
HEAD_DIM = D_MODEL // N_HEADS
ROT_DIM = HEAD_DIM // 4
ROPE_THETA = 500000.0
MOBA_BLOCK = 256
MOBA_TOPK = 3
Q_CHUNK = 16
CONV_WIDTH = 3
D_FF = ((8 * D_MODEL // 3 + 255) // 256) * 256
RMS_EPS = 1e-6
NEG = -1e30

kernel_name = "hybrid_shortconv_moba_macaron"


def rms_norm(x, g):
    x32 = x.astype(jnp.float32)
    y = x32 * lax.rsqrt(jnp.mean(x32 * x32, axis=-1, keepdims=True) + RMS_EPS)
    return (y * g.astype(jnp.float32)).astype(x.dtype)


def swiglu(h, w_in, w_out):
    gu = h @ w_in
    g, u = jnp.split(gu, 2, axis=-1)
    return (jax.nn.silu(g) * u) @ w_out


def rope_tables(seq):
    inv_freq = ROPE_THETA ** (-jnp.arange(0, ROT_DIM, 2, dtype=jnp.float32) / ROT_DIM)
    ang = jnp.arange(seq, dtype=jnp.float32)[:, None] * inv_freq[None, :]
    return jnp.cos(ang), jnp.sin(ang)


def apply_partial_rope(t, cos, sin):
    half = ROT_DIM // 2
    t1 = t[..., :half].astype(jnp.float32)
    t2 = t[..., half:ROT_DIM].astype(jnp.float32)
    r1 = t1 * cos - t2 * sin
    r2 = t2 * cos + t1 * sin
    return jnp.concatenate([r1.astype(t.dtype), r2.astype(t.dtype), t[..., ROT_DIM:]], axis=-1)


def short_conv_mixer(h, w_in, conv_w, w_out):
    bcu = h @ w_in
    b_gate, c_gate, u = jnp.split(bcu, 3, axis=-1)
    z = c_gate * u
    z = lax.conv_general_dilated(
        z, conv_w[:, None, :].astype(z.dtype),
        window_strides=(1,), padding=[(CONV_WIDTH - 1, 0)],
        dimension_numbers=('NWC', 'WIO', 'NWC'),
        feature_group_count=D_MODEL)
    return (b_gate * z) @ w_out


def moba_attention(h, w_qkv, w_o, cos, sin):
    bsz, seq, _ = h.shape
    qkv = h @ w_qkv
    q, k, v = jnp.split(qkv, 3, axis=-1)
    to_heads = lambda t: t.reshape(bsz, seq, N_HEADS, HEAD_DIM).transpose(0, 2, 1, 3)
    q = apply_partial_rope(to_heads(q), cos, sin)
    k = apply_partial_rope(to_heads(k), cos, sin)
    v = to_heads(v)

    n_blk = -(-seq // MOBA_BLOCK)
    s_pad = n_blk * MOBA_BLOCK
    pad = ((0, 0), (0, 0), (0, s_pad - seq), (0, 0))
    kb = jnp.pad(k, pad).reshape(bsz, N_HEADS, n_blk, MOBA_BLOCK, HEAD_DIM)
    vb = jnp.pad(v, pad).reshape(bsz, N_HEADS, n_blk, MOBA_BLOCK, HEAD_DIM)

    q_blk = jnp.arange(seq) // MOBA_BLOCK
    n_sel = min(MOBA_TOPK, n_blk - 1)
    scale = HEAD_DIM ** -0.5

    if n_sel > 0:
        k_mean = jnp.mean(kb.astype(jnp.float32), axis=3)
        gate = jnp.einsum('bhsd,bhnd->bhsn', q.astype(jnp.float32), k_mean)
        past = jnp.arange(n_blk)[None, :] < q_blk[:, None]
        gate = jnp.where(past, gate, jnp.finfo(jnp.float32).min)
        _, sel_idx = lax.top_k(gate, n_sel)
        slot_ok = jnp.arange(n_sel)[None, :] < q_blk[:, None]
        b_ix = jnp.arange(bsz)[:, None, None, None]
        h_ix = jnp.arange(N_HEADS)[None, :, None, None]

    def attend_chunk(c):
        t0 = c * Q_CHUNK
        qc = lax.dynamic_slice_in_dim(q, t0, Q_CHUNK, axis=2)
        own = t0 // MOBA_BLOCK
        k_own = lax.dynamic_index_in_dim(kb, own, axis=2, keepdims=False)
        v_own = lax.dynamic_index_in_dim(vb, own, axis=2, keepdims=False)
        q_pos = t0 + jnp.arange(Q_CHUNK)
        k_pos = own * MOBA_BLOCK + jnp.arange(MOBA_BLOCK)
        s_own = jnp.einsum('bhqd,bhkd->bhqk', qc, k_own).astype(jnp.float32) * scale
        s_own = jnp.where(k_pos[None, :] <= q_pos[:, None], s_own, NEG)
        if n_sel == 0:
            p = jax.nn.softmax(s_own, axis=-1).astype(v.dtype)
            return jnp.einsum('bhqk,bhkd->bhqd', p, v_own)
        idx = lax.dynamic_slice_in_dim(sel_idx, t0, Q_CHUNK, axis=2)
        ok = lax.dynamic_slice_in_dim(slot_ok, t0, Q_CHUNK, axis=0)
        k_sel = kb[b_ix, h_ix, idx]
        v_sel = vb[b_ix, h_ix, idx]
        s_sel = jnp.einsum('bhqd,bhqnkd->bhqnk', qc, k_sel).astype(jnp.float32) * scale
        s_sel = jnp.where(ok[:, :, None], s_sel, NEG)
        s_all = jnp.concatenate([s_sel.reshape(bsz, N_HEADS, Q_CHUNK, n_sel * MOBA_BLOCK), s_own], axis=-1)
        p = jax.nn.softmax(s_all, axis=-1).astype(v.dtype)
        p_sel = p[..., :n_sel * MOBA_BLOCK].reshape(bsz, N_HEADS, Q_CHUNK, n_sel, MOBA_BLOCK)
        p_own = p[..., n_sel * MOBA_BLOCK:]
        return (jnp.einsum('bhqnk,bhqnkd->bhqd', p_sel, v_sel)
                + jnp.einsum('bhqk,bhkd->bhqd', p_own, v_own))

    outs = lax.map(attend_chunk, jnp.arange(seq // Q_CHUNK))
    o = outs.transpose(1, 0, 3, 2, 4).reshape(bsz, seq, D_MODEL)
    return o @ w_o


def setup_inputs(seed: int = 0) -> dict:
    key = jax.random.key(seed)
    ks = jax.random.split(key, 10)
    f32 = jnp.float32
    x = jax.random.normal(ks[0], (BATCH, SEQ, D_MODEL), f32)
    norm_g = 1.0 + 0.05 * jax.random.normal(ks[1], (DEPTH, 6, D_MODEL), f32)
    ffn_w_in = jax.random.normal(ks[2], (DEPTH, 2, D_MODEL, 2 * D_FF), f32) * D_MODEL ** -0.5
    ffn_w_out = jax.random.normal(ks[3], (DEPTH, 2, D_FF, D_MODEL), f32) * D_FF ** -0.5
    conv_w_in = jax.random.normal(ks[4], (N_CONV_LAYERS, D_MODEL, 3 * D_MODEL), f32) * D_MODEL ** -0.5
    conv_w = jax.random.normal(ks[5], (N_CONV_LAYERS, CONV_WIDTH, D_MODEL), f32) * CONV_WIDTH ** -0.5
    conv_w_out = jax.random.normal(ks[6], (N_CONV_LAYERS, D_MODEL, D_MODEL), f32) * D_MODEL ** -0.5
    attn_w_qkv = jax.random.normal(ks[7], (N_ATTN_LAYERS, D_MODEL, 3 * D_MODEL), f32) * D_MODEL ** -0.5
    attn_w_o = jax.random.normal(ks[8], (N_ATTN_LAYERS, D_MODEL, D_MODEL), f32) * D_MODEL ** -0.5
    return {"x": x, "norm_g": norm_g, "ffn_w_in": ffn_w_in, "ffn_w_out": ffn_w_out,
            "conv_w_in": conv_w_in, "conv_w": conv_w, "conv_w_out": conv_w_out,
            "attn_w_qkv": attn_w_qkv, "attn_w_o": attn_w_o}


def reference(x, norm_g, ffn_w_in, ffn_w_out, conv_w_in, conv_w, conv_w_out, attn_w_qkv, attn_w_o):
    seq = x.shape[1]
    cos, sin = rope_tables(seq)
    for i in range(DEPTH):
        g = norm_g[i]
        x = x + 0.5 * rms_norm(swiglu(rms_norm(x, g[0]), ffn_w_in[i, 0], ffn_w_out[i, 0]), g[1])
        h = rms_norm(x, g[2])
        if i % N_MIXERS == 0:
            j = i // N_MIXERS
            m = short_conv_mixer(h, conv_w_in[j], conv_w[j], conv_w_out[j])
        else:
            j = i // N_MIXERS
            m = moba_attention(h, attn_w_qkv[j], attn_w_o[j], cos, sin)
        x = x + rms_norm(m, g[3])
        x = x + 0.5 * rms_norm(swiglu(rms_norm(x, g[4]), ffn_w_in[i, 1], ffn_w_out[i, 1]), g[5])
    return x
```

```python
import functools

import jax
import jax.numpy as jnp
from jax import lax
from jax.experimental import pallas as pl
from jax.experimental.pallas import tpu as pltpu

HEAD_DIM = 128
ROT_DIM = HEAD_DIM // 4
ROPE_THETA = 500000.0
MOBA_BLOCK = 256
MOBA_TOPK = 3
CONV_WIDTH = 3
RMS_EPS = 1e-6
NEG = -1e30

LANES = 128
SUBLANES = 8
V7X_VMEM_LIMIT_BYTES = 56 * 1024 * 1024

F32 = jnp.float32
BF16 = jnp.bfloat16


def _pick_tile(n, target, multiple):
    t = min(n, target)
    t -= t % multiple
    while t > multiple and n % t:
        t -= multiple
    assert t > 0 and n % t == 0, (n, target, multiple)
    return t


def _rms(x, g):
    ms = jnp.mean(x * x, axis=-1, keepdims=True)
    return (x * lax.rsqrt(ms + RMS_EPS)) * g


def _params(sem):
    return pltpu.CompilerParams(dimension_semantics=sem,
                                vmem_limit_bytes=V7X_VMEM_LIMIT_BYTES)


def _ffn_kernel(x_ref, gpre_ref, gpost_ref, wg_ref, wu_ref, wo_ref, o_ref, h_ref, acc_ref):
    j = pl.program_id(1)

    @pl.when(j == 0)
    def _():
        h_ref[...] = _rms(x_ref[...], gpre_ref[...]).astype(BF16)
        acc_ref[...] = jnp.zeros_like(acc_ref)

    h = h_ref[...]
    g = jnp.dot(h, wg_ref[...], preferred_element_type=F32)
    u = jnp.dot(h, wu_ref[...], preferred_element_type=F32)
    a = (jax.nn.silu(g) * u).astype(BF16)
    acc_ref[...] += jnp.dot(a, wo_ref[...], preferred_element_type=F32)

    @pl.when(j == pl.num_programs(1) - 1)
    def _():
        o_ref[...] = x_ref[...] + 0.5 * _rms(acc_ref[...], gpost_ref[...])


def _ffn(x, g_pre, g_post, w_in, w_out):
    t, d = x.shape
    f = w_out.shape[0]
    tm = _pick_tile(t, 512, SUBLANES)
    tf = _pick_tile(f, 512, LANES)
    nj = f // tf
    return pl.pallas_call(
        _ffn_kernel,
        grid=(t // tm, nj),
        in_specs=[
            pl.BlockSpec((tm, d), lambda i, j: (i, 0)),
            pl.BlockSpec((1, d), lambda i, j: (0, 0)),
            pl.BlockSpec((1, d), lambda i, j: (0, 0)),
            pl.BlockSpec((d, tf), lambda i, j: (0, j)),
            pl.BlockSpec((d, tf), lambda i, j: (0, nj + j)),
            pl.BlockSpec((tf, d), lambda i, j: (j, 0)),
        ],
        out_specs=pl.BlockSpec((tm, d), lambda i, j: (i, 0)),
        out_shape=jax.ShapeDtypeStruct((t, d), F32),
        scratch_shapes=[pltpu.VMEM((tm, d), BF16), pltpu.VMEM((tm, d), F32)],
        compiler_params=_params(("arbitrary", "arbitrary")),
        name="ffn_half_step",
    )(x, g_pre, g_post, w_in, w_in, w_out)


def _conv_kernel(tiles_per_seq, x_ref, gpre_ref, gpost_ref, wb_ref, wc_ref, wu_ref, cw_ref,
                 wo_ref, o_ref, h_ref, acc_ref, zbuf_ref, halo_ref):
    i = pl.program_id(0)
    j = pl.program_id(1)
    tm = x_ref.shape[0]

    @pl.when(j == 0)
    def _():
        h_ref[...] = _rms(x_ref[...], gpre_ref[...]).astype(BF16)
        acc_ref[...] = jnp.zeros_like(acc_ref)

    h = h_ref[...]
    b = jnp.dot(h, wb_ref[...], preferred_element_type=F32)
    c = jnp.dot(h, wc_ref[...], preferred_element_type=F32)
    u = jnp.dot(h, wu_ref[...], preferred_element_type=F32)
    z = c * u

    zbuf_ref[pl.ds(SUBLANES, tm), :] = z
    prev = halo_ref[j]
    zbuf_ref[pl.ds(0, SUBLANES), :] = jnp.where(i % tiles_per_seq == 0, jnp.zeros_like(prev), prev)
    halo_ref[j] = z[tm - SUBLANES:, :]

    cw = cw_ref[...]
    zc = (cw[0:1, :] * zbuf_ref[pl.ds(SUBLANES - 2, tm), :]
          + cw[1:2, :] * zbuf_ref[pl.ds(SUBLANES - 1, tm), :]
          + cw[2:3, :] * z)
    y = (b * zc).astype(BF16)
    acc_ref[...] += jnp.dot(y, wo_ref[...], preferred_element_type=F32)

    @pl.when(j == pl.num_programs(1) - 1)
    def _():
        o_ref[...] = x_ref[...] + _rms(acc_ref[...], gpost_ref[...])


def _conv_mixer(x, seq, g_pre, g_post, w_in, conv_w, w_out):
    t, d = x.shape
    tm = _pick_tile(seq, 512, SUBLANES)
    tn = _pick_tile(d, 512, LANES)
    nj = d // tn
    return pl.pallas_call(
        functools.partial(_conv_kernel, seq // tm),
        grid=(t // tm, nj),
        in_specs=[
            pl.BlockSpec((tm, d), lambda i, j: (i, 0)),
            pl.BlockSpec((1, d), lambda i, j: (0, 0)),
            pl.BlockSpec((1, d), lambda i, j: (0, 0)),
            pl.BlockSpec((d, tn), lambda i, j: (0, j)),
            pl.BlockSpec((d, tn), lambda i, j: (0, nj + j)),
            pl.BlockSpec((d, tn), lambda i, j: (0, 2 * nj + j)),
            pl.BlockSpec((CONV_WIDTH, tn), lambda i, j: (0, j)),
            pl.BlockSpec((tn, d), lambda i, j: (j, 0)),
        ],
        out_specs=pl.BlockSpec((tm, d), lambda i, j: (i, 0)),
        out_shape=jax.ShapeDtypeStruct((t, d), F32),
        scratch_shapes=[
            pltpu.VMEM((tm, d), BF16),
            pltpu.VMEM((tm, d), F32),
            pltpu.VMEM((tm + SUBLANES, tn), F32),
            pltpu.VMEM((nj, SUBLANES, tn), F32),
        ],
        compiler_params=_params(("arbitrary", "arbitrary")),
        name="short_conv_mixer",
    )(x, g_pre, g_post, w_in, w_in, w_in, conv_w, w_out)


def _qkv_kernel(x_ref, gpre_ref, w_ref, cos_ref, sin_ref, o_ref, ksum_ref, h_ref):
    j = pl.program_id(1)
    tm, d = x_ref.shape

    @pl.when(j == 0)
    def _():
        h_ref[...] = _rms(x_ref[...], gpre_ref[...]).astype(BF16)

    r = jnp.dot(h_ref[...], w_ref[...], preferred_element_type=F32)

    @pl.when(j == 2)
    def _():
        o_ref[...] = r.astype(BF16)

    @pl.when(j == 0)
    def _():
        ksum_ref[...] = jnp.zeros_like(ksum_ref)

    @pl.when(j < 2)
    def _():
        half = ROT_DIM // 2
        nb = tm // MOBA_BLOCK
        lane = lax.broadcasted_iota(jnp.int32, (tm, HEAD_DIM), 1)
        cos = cos_ref[...]
        sin = sin_ref[...]
        for hd in range(d // HEAD_DIM):
            cols = slice(hd * HEAD_DIM, (hd + 1) * HEAD_DIM)
            th = r[:, cols]
            partner = jnp.where(lane < half,
                                pltpu.roll(th, HEAD_DIM - half, 1),
                                pltpu.roll(th, half, 1))
            rot = th * cos + partner * sin
            o_ref[:, cols] = rot.astype(BF16)
            ksum_ref[0, 0:nb, cols] = jnp.sum(rot.reshape(nb, MOBA_BLOCK, HEAD_DIM), axis=1)


def _rope_tables(seq):
    half = ROT_DIM // 2
    inv_freq = ROPE_THETA ** (-jnp.arange(0, ROT_DIM, 2, dtype=F32) / ROT_DIM)
    ang = jnp.arange(seq, dtype=F32)[:, None] * inv_freq[None, :]
    cos, sin = jnp.cos(ang), jnp.sin(ang)
    pad = HEAD_DIM - ROT_DIM
    cos_t = jnp.concatenate([cos, cos, jnp.ones((seq, pad), F32)], axis=-1)
    sin_t = jnp.concatenate([-sin, sin, jnp.zeros((seq, pad), F32)], axis=-1)
    return cos_t, sin_t


def _qkv_proj(x, seq, g_pre, w_qkv):
    t, d = x.shape
    tm = _pick_tile(seq, 512, MOBA_BLOCK)
    assert tm // MOBA_BLOCK <= SUBLANES
    tiles_per_seq = seq // tm
    cos_t, sin_t = _rope_tables(seq)
    qkv, ksum = pl.pallas_call(
        _qkv_kernel,
        grid=(t // tm, 3),
        in_specs=[
            pl.BlockSpec((tm, d), lambda i, j: (i, 0)),
            pl.BlockSpec((1, d), lambda i, j: (0, 0)),
            pl.BlockSpec((d, d), lambda i, j: (0, j)),
            pl.BlockSpec((tm, HEAD_DIM), lambda i, j: (i % tiles_per_seq, 0)),
            pl.BlockSpec((tm, HEAD_DIM), lambda i, j: (i % tiles_per_seq, 0)),
        ],
        out_specs=[
            pl.BlockSpec((tm, d), lambda i, j: (i, j)),
            pl.BlockSpec((1, SUBLANES, d), lambda i, j: (i, 0, 0)),
        ],
        out_shape=[
            jax.ShapeDtypeStruct((t, 3 * d), BF16),
            jax.ShapeDtypeStruct((t // tm, SUBLANES, d), F32),
        ],
        scratch_shapes=[pltpu.VMEM((tm, d), BF16)],
        compiler_params=_params(("arbitrary", "arbitrary")),
        name="qkv_rope_proj",
    )(x, g_pre, w_qkv, cos_t, sin_t)
    ksum = ksum[:, :tm // MOBA_BLOCK, :].reshape(t // MOBA_BLOCK, d)
    return qkv, ksum


def _moba_kernel(q_ref, k_ref, v_ref, ksum_ref, o_ref, s_ref):
    seq = q_ref.shape[0]
    nb = seq // MOBA_BLOCK
    blk = MOBA_BLOCK
    scale = HEAD_DIM ** -0.5
    nt = (((1,), (1,)), ((), ()))
    tn = (((0,), (0,)), ((), ()))

    kmean = (ksum_ref[...] * (1.0 / blk)).astype(BF16)
    gate = lax.dot_general(kmean, q_ref[...], nt, preferred_element_type=F32)
    n_idx = lax.broadcasted_iota(jnp.int32, (nb, seq), 0)
    q_blk = lax.broadcasted_iota(jnp.int32, (nb, seq), 1) // blk
    rank = jnp.zeros((nb, seq), jnp.int32)
    for m in range(nb - 1):
        gm = gate[m:m + 1, :]
        beats = (gm > gate) | ((gm == gate) & (m < n_idx))
        rank = rank + jnp.where(beats & (m < q_blk), 1, 0)
    sel = jnp.where((n_idx < q_blk) & (rank < MOBA_TOPK), 1.0, 0.0).astype(F32)

    key_i = lax.broadcasted_iota(jnp.int32, (blk, blk), 0)
    qry_i = lax.broadcasted_iota(jnp.int32, (blk, blk), 1)
    causal = key_i <= qry_i

    for i in range(nb):
        qi = q_ref[i * blk:(i + 1) * blk, :]
        m_run = None
        for jb in range(i + 1):
            kj = k_ref[jb * blk:(jb + 1) * blk, :]
            s = lax.dot_general(kj, qi, nt, preferred_element_type=F32) * scale
            if jb == i:
                s = jnp.where(causal, s, NEG)
            else:
                s = jnp.where(sel[jb:jb + 1, i * blk:(i + 1) * blk] > 0.5, s, NEG)
            s_ref[jb] = s
            m_run = s if m_run is None else jnp.maximum(m_run, s)
        m = jnp.max(m_run, axis=0, keepdims=True)
        l_run = None
        acc = None
        for jb in range(i + 1):
            p = jnp.exp(s_ref[jb] - m)
            l_run = p if l_run is None else l_run + p
            vj = v_ref[jb * blk:(jb + 1) * blk, :]
            pv = lax.dot_general(vj, p.astype(BF16), tn, preferred_element_type=F32)
            acc = pv if acc is None else acc + pv
        l = jnp.sum(l_run, axis=0, keepdims=True)
        o_ref[i * blk:(i + 1) * blk, :] = (acc / l).T.astype(BF16)


def _moba_attention(qkv, ksum, bsz, seq):
    t, d3 = qkv.shape
    d = d3 // 3
    nh = d // HEAD_DIM
    nb = seq // MOBA_BLOCK
    return pl.pallas_call(
        _moba_kernel,
        grid=(bsz, nh),
        in_specs=[
            pl.BlockSpec((seq, HEAD_DIM), lambda b, h: (b, h)),
            pl.BlockSpec((seq, HEAD_DIM), lambda b, h: (b, nh + h)),
            pl.BlockSpec((seq, HEAD_DIM), lambda b, h: (b, 2 * nh + h)),
            pl.BlockSpec((nb, HEAD_DIM), lambda b, h: (b, h)),
        ],
        out_specs=pl.BlockSpec((seq, HEAD_DIM), lambda b, h: (b, h)),
        out_shape=jax.ShapeDtypeStruct((t, d), BF16),
        scratch_shapes=[pltpu.VMEM((nb, MOBA_BLOCK, MOBA_BLOCK), F32)],
        compiler_params=_params(("arbitrary", "arbitrary")),
        name="moba_attention",
    )(qkv, qkv, qkv, ksum)


def _oproj_kernel(x_ref, o_ref, w_ref, gpost_ref, out_ref):
    m = jnp.dot(o_ref[...], w_ref[...], preferred_element_type=F32)
    out_ref[...] = x_ref[...] + _rms(m, gpost_ref[...])


def _out_proj(x, o, w_o, g_post):
    t, d = x.shape
    tm = _pick_tile(t, 512, SUBLANES)
    return pl.pallas_call(
        _oproj_kernel,
        grid=(t // tm,),
        in_specs=[
            pl.BlockSpec((tm, d), lambda i: (i, 0)),
            pl.BlockSpec((tm, d), lambda i: (i, 0)),
            pl.BlockSpec((d, d), lambda i: (0, 0)),
            pl.BlockSpec((1, d), lambda i: (0, 0)),
        ],
        out_specs=pl.BlockSpec((tm, d), lambda i: (i, 0)),
        out_shape=jax.ShapeDtypeStruct((t, d), F32),
        compiler_params=_params(("arbitrary",)),
        name="attn_out_proj",
    )(x, o, w_o, g_post)


def kernel(x, norm_g, ffn_w_in, ffn_w_out, conv_w_in, conv_w, conv_w_out, attn_w_qkv, attn_w_o):
    bsz, seq, d = x.shape
    depth = norm_g.shape[0]
    assert d % HEAD_DIM == 0 and seq % MOBA_BLOCK == 0
    assert min(MOBA_TOPK, seq // MOBA_BLOCK - 1) == MOBA_TOPK
    h = x.reshape(bsz * seq, d)
    for i in range(depth):
        g = norm_g[i][:, None, :]
        h = _ffn(h, g[0], g[1], ffn_w_in[i, 0].astype(BF16), ffn_w_out[i, 0].astype(BF16))
        jm = i // 2
        if i % 2 == 0:
            h = _conv_mixer(h, seq, g[2], g[3], conv_w_in[jm].astype(BF16), conv_w[jm],
                            conv_w_out[jm].astype(BF16))
        else:
            qkv, ksum = _qkv_proj(h, seq, g[2], attn_w_qkv[jm].astype(BF16))
            o = _moba_attention(qkv, ksum, bsz, seq)
            h = _out_proj(h, o, attn_w_o[jm].astype(BF16), g[3])
        h = _ffn(h, g[4], g[5], ffn_w_in[i, 1].astype(BF16), ffn_w_out[i, 1].astype(BF16))
    return h.reshape(bsz, seq, d)
```

```python
import functools

import jax
import jax.numpy as jnp
from jax import lax
from jax.experimental import pallas as pl
from jax.experimental.pallas import tpu as pltpu

HEAD_DIM = 128
ROT_DIM = HEAD_DIM // 4
ROPE_THETA = 500000.0
MOBA_BLOCK = 256
MOBA_TOPK = 3
CONV_WIDTH = 3
RMS_EPS = 1e-6
NEG = -1e30
LOG2_E = 1.4426950408889634

LANES = 128
SUBLANES = 8
V7X_VMEM_LIMIT_BYTES = 56 * 1024 * 1024

F32 = jnp.float32
BF16 = jnp.bfloat16


def _pick_tile(n, target, multiple):
    t = min(n, target)
    t -= t % multiple
    while t > multiple and n % t:
        t -= multiple
    assert t > 0 and n % t == 0, (n, target, multiple)
    return t


def _rms(x, g):
    ms = jnp.mean(x * x, axis=-1, keepdims=True)
    return (x * lax.rsqrt(ms + RMS_EPS)) * g


def _params(sem):
    return pltpu.CompilerParams(dimension_semantics=sem,
                                vmem_limit_bytes=V7X_VMEM_LIMIT_BYTES)


def _stacked_spec(lead, block, index_fn):
    lead = tuple(lead)
    return pl.BlockSpec((None,) * len(lead) + tuple(block), lambda *g: lead + tuple(index_fn(*g)))


BF16_SUBLANES = 16


def _slices_per_tile(tm, nj):
    n = min(nj, tm // BF16_SUBLANES)
    while tm % n or (tm // n) % BF16_SUBLANES:
        n -= 1
    return n


def _ffn_kernel(n_tiles, n_slices, xn_ref, xf_ref, gpre_ref, gpost_ref, wg_ref, wu_ref, wo_ref,
                o_ref, h_ref, acc_ref):
    r = pl.program_id(0)
    j = pl.program_id(1)
    sr = xn_ref.shape[0]
    rows = pl.ds(pl.multiple_of(jnp.minimum(j, n_slices - 1) * sr, sr), sr)
    norm_slot = r % 2
    mm_slot = 1 - norm_slot

    @pl.when((r == 0) & (j == 0))
    def _():
        acc_ref[...] = jnp.zeros_like(acc_ref)

    def finalize_slice():
        o_ref[...] = xf_ref[...] + 0.5 * _rms(acc_ref[norm_slot, rows, :], gpost_ref[...])

    def norm_slice():
        h_ref[norm_slot, rows, :] = _rms(xn_ref[...], gpre_ref[...]).astype(BF16)

    has_matmul = (r >= 1) & (r <= n_tiles)

    @pl.when(has_matmul)
    def _():
        finalize_slice()
        h = h_ref[mm_slot]
        g = jnp.dot(h, wg_ref[...], preferred_element_type=F32)
        u = jnp.dot(h, wu_ref[...], preferred_element_type=F32)
        a = (jax.nn.silu(g) * u).astype(BF16)
        acc_ref[mm_slot] = (jnp.dot(a, wo_ref[...], preferred_element_type=F32)
                            + jnp.where(j == 0, 0.0, acc_ref[mm_slot]))
        norm_slice()

    @pl.when(jnp.logical_not(has_matmul))
    def _():
        finalize_slice()
        norm_slice()


def _ffn(x, g_pre, g_post, w_in, w_out, lead):
    t, d = x.shape
    f = w_out.shape[-2]
    tm = _pick_tile(t, 1024, BF16_SUBLANES)
    tf = _pick_tile(f, 512, LANES)
    nj = f // tf
    n_tiles = t // tm
    n_slices = _slices_per_tile(tm, nj)
    sr = tm // n_slices

    def norm_slice(r, j):
        return (jnp.minimum(r, n_tiles - 1) * n_slices + jnp.minimum(j, n_slices - 1), 0)

    def final_slice(r, j):
        return (jnp.where(r < 2, 0, (r - 2) * n_slices + jnp.minimum(j, n_slices - 1)), 0)

    def chunk(r, j):
        return jnp.where((r >= 1) & (r <= n_tiles), j, 0)

    return pl.pallas_call(
        functools.partial(_ffn_kernel, n_tiles, n_slices),
        grid=(n_tiles + 2, nj),
        in_specs=[
            pl.BlockSpec((sr, d), norm_slice),
            pl.BlockSpec((sr, d), final_slice),
            pl.BlockSpec((1, d), lambda r, j: (0, 0)),
            pl.BlockSpec((1, d), lambda r, j: (0, 0)),
            _stacked_spec(lead, (d, tf), lambda r, j: (0, chunk(r, j))),
            _stacked_spec(lead, (d, tf), lambda r, j: (0, nj + chunk(r, j))),
            _stacked_spec(lead, (tf, d), lambda r, j: (chunk(r, j), 0)),
        ],
        out_specs=pl.BlockSpec((sr, d), final_slice),
        out_shape=jax.ShapeDtypeStruct((t, d), F32),
        scratch_shapes=[pltpu.VMEM((2, tm, d), BF16), pltpu.VMEM((2, tm, d), F32)],
        compiler_params=_params(("arbitrary", "arbitrary")),
        name="ffn_half_step",
    )(x, x, g_pre, g_post, w_in, w_in, w_out)


def _conv_kernel(tiles_per_seq, x_ref, gpre_ref, gpost_ref, wb_ref, wc_ref, wu_ref, cw_ref,
                 wo_ref, o_ref, h_ref, acc_ref, zbuf_ref, halo_ref):
    i = pl.program_id(0)
    j = pl.program_id(1)
    tm = x_ref.shape[0]

    @pl.when(j == 0)
    def _():
        h_ref[...] = _rms(x_ref[...], gpre_ref[...]).astype(BF16)
        acc_ref[...] = jnp.zeros_like(acc_ref)

    h = h_ref[...]
    b = jnp.dot(h, wb_ref[...], preferred_element_type=F32)
    c = jnp.dot(h, wc_ref[...], preferred_element_type=F32)
    u = jnp.dot(h, wu_ref[...], preferred_element_type=F32)
    z = c * u

    zbuf_ref[pl.ds(SUBLANES, tm), :] = z
    prev = halo_ref[j]
    zbuf_ref[pl.ds(0, SUBLANES), :] = jnp.where(i % tiles_per_seq == 0, jnp.zeros_like(prev), prev)
    halo_ref[j] = z[tm - SUBLANES:, :]

    cw = cw_ref[...]
    zc = (cw[0:1, :] * zbuf_ref[pl.ds(SUBLANES - 2, tm), :]
          + cw[1:2, :] * zbuf_ref[pl.ds(SUBLANES - 1, tm), :]
          + cw[2:3, :] * z)
    y = (b * zc).astype(BF16)
    acc_ref[...] += jnp.dot(y, wo_ref[...], preferred_element_type=F32)

    @pl.when(j == pl.num_programs(1) - 1)
    def _():
        o_ref[...] = x_ref[...] + _rms(acc_ref[...], gpost_ref[...])


def _conv_mixer(x, seq, g_pre, g_post, w_in, conv_w, w_out, lead):
    t, d = x.shape
    tm = _pick_tile(seq, 512, SUBLANES)
    tn = _pick_tile(d, 512, LANES)
    nj = d // tn
    return pl.pallas_call(
        functools.partial(_conv_kernel, seq // tm),
        grid=(t // tm, nj),
        in_specs=[
            pl.BlockSpec((tm, d), lambda i, j: (i, 0)),
            pl.BlockSpec((1, d), lambda i, j: (0, 0)),
            pl.BlockSpec((1, d), lambda i, j: (0, 0)),
            _stacked_spec(lead, (d, tn), lambda i, j: (0, j)),
            _stacked_spec(lead, (d, tn), lambda i, j: (0, nj + j)),
            _stacked_spec(lead, (d, tn), lambda i, j: (0, 2 * nj + j)),
            _stacked_spec(lead, (CONV_WIDTH, tn), lambda i, j: (0, j)),
            _stacked_spec(lead, (tn, d), lambda i, j: (j, 0)),
        ],
        out_specs=pl.BlockSpec((tm, d), lambda i, j: (i, 0)),
        out_shape=jax.ShapeDtypeStruct((t, d), F32),
        scratch_shapes=[
            pltpu.VMEM((tm, d), BF16),
            pltpu.VMEM((tm, d), F32),
            pltpu.VMEM((tm + SUBLANES, tn), F32),
            pltpu.VMEM((nj, SUBLANES, tn), F32),
        ],
        compiler_params=_params(("arbitrary", "arbitrary")),
        name="short_conv_mixer",
    )(x, g_pre, g_post, w_in, w_in, w_in, conv_w, w_out)


MXU_WIDTH = 256


def _qkv_pass_col(j):
    return (2 * j) % 3


def _qkv_kernel(x_ref, gpre_ref, w_ref, cos_ref, sin_ref, o_ref, ksum_ref, h_ref):
    j = pl.program_id(1)
    tm, d = x_ref.shape
    half = ROT_DIM // 2
    nb = tm // MOBA_BLOCK

    @pl.when(j == 0)
    def _():
        h_ref[...] = _rms(x_ref[...], gpre_ref[...]).astype(BF16)

    rotary = j != 1
    cos = jnp.where(rotary, cos_ref[...], 1.0)
    sin = jnp.where(rotary, sin_ref[...], 0.0)
    lane = lax.broadcasted_iota(jnp.int32, (tm, HEAD_DIM), 1)
    pad_rows = jnp.zeros((SUBLANES - nb, HEAD_DIM), F32)
    h = h_ref[...]
    for c in range(d // MXU_WIDTH):
        r = jnp.dot(h, w_ref[:, c * MXU_WIDTH:(c + 1) * MXU_WIDTH], preferred_element_type=F32)
        for hh in range(MXU_WIDTH // HEAD_DIM):
            th = r[:, hh * HEAD_DIM:(hh + 1) * HEAD_DIM]
            partner = jnp.where(lane < half,
                                pltpu.roll(th, HEAD_DIM - half, 1),
                                pltpu.roll(th, half, 1))
            rot = th * cos + partner * sin
            cols = slice(c * MXU_WIDTH + hh * HEAD_DIM, c * MXU_WIDTH + (hh + 1) * HEAD_DIM)
            o_ref[:, cols] = rot.astype(BF16)
            sums = jnp.sum(rot.reshape(nb, MOBA_BLOCK, HEAD_DIM), axis=1)
            ksum_ref[0, :, cols] = jnp.concatenate([sums, pad_rows], axis=0) if nb < SUBLANES else sums


def _rope_tables(seq):
    half = ROT_DIM // 2
    inv_freq = ROPE_THETA ** (-jnp.arange(0, ROT_DIM, 2, dtype=F32) / ROT_DIM)
    ang = jnp.arange(seq, dtype=F32)[:, None] * inv_freq[None, :]
    cos, sin = jnp.cos(ang), jnp.sin(ang)
    pad = HEAD_DIM - ROT_DIM
    cos_t = jnp.concatenate([cos, cos, jnp.ones((seq, pad), F32)], axis=-1)
    sin_t = jnp.concatenate([-sin, sin, jnp.zeros((seq, pad), F32)], axis=-1)
    return cos_t, sin_t


def _qkv_proj(x, seq, g_pre, w_qkv, lead):
    t, d = x.shape
    tm = _pick_tile(seq, 512, MOBA_BLOCK)
    assert tm // MOBA_BLOCK <= SUBLANES
    tiles_per_seq = seq // tm
    cos_t, sin_t = _rope_tables(seq)
    qkv, ksum = pl.pallas_call(
        _qkv_kernel,
        grid=(t // tm, 3),
        in_specs=[
            pl.BlockSpec((tm, d), lambda i, j: (i, 0)),
            pl.BlockSpec((1, d), lambda i, j: (0, 0)),
            _stacked_spec(lead, (d, d), lambda i, j: (0, _qkv_pass_col(j))),
            pl.BlockSpec((tm, HEAD_DIM), lambda i, j: (i % tiles_per_seq, 0)),
            pl.BlockSpec((tm, HEAD_DIM), lambda i, j: (i % tiles_per_seq, 0)),
        ],
        out_specs=[
            pl.BlockSpec((tm, d), lambda i, j: (i, _qkv_pass_col(j))),
            pl.BlockSpec((1, SUBLANES, d), lambda i, j: (i, 0, 0)),
        ],
        out_shape=[
            jax.ShapeDtypeStruct((t, 3 * d), BF16),
            jax.ShapeDtypeStruct((t // tm, SUBLANES, d), F32),
        ],
        scratch_shapes=[pltpu.VMEM((tm, d), BF16)],
        compiler_params=_params(("arbitrary", "arbitrary")),
        name="qkv_rope_proj",
    )(x, g_pre, w_qkv, cos_t, sin_t)
    ksum = ksum[:, :tm // MOBA_BLOCK, :].reshape(t // MOBA_BLOCK, d)
    return qkv, ksum


def _moba_kernel(q_ref, k_ref, v_ref, ksum_ref, o_ref, s_ref):
    seq = q_ref.shape[0]
    nb = seq // MOBA_BLOCK
    blk = MOBA_BLOCK
    scale = HEAD_DIM ** -0.5
    nt = (((1,), (1,)), ((), ()))
    tn = (((0,), (0,)), ((), ()))

    kmean = (ksum_ref[...] * (1.0 / blk)).astype(BF16)
    gate = lax.dot_general(kmean, q_ref[...], nt, preferred_element_type=F32)
    n_idx = lax.broadcasted_iota(jnp.int32, (nb, seq), 0)
    q_blk = lax.broadcasted_iota(jnp.int32, (nb, seq), 1) // blk
    rank = jnp.zeros((nb, seq), jnp.int32)
    for m in range(nb - 1):
        gm = gate[m:m + 1, :]
        beats = (gm > gate) | ((gm == gate) & (m < n_idx))
        rank = rank + jnp.where(beats & (m < q_blk), 1, 0)
    sel = jnp.where((n_idx < q_blk) & (rank < MOBA_TOPK), 1.0, 0.0).astype(F32)

    key_i = lax.broadcasted_iota(jnp.int32, (blk, blk), 0)
    qry_i = lax.broadcasted_iota(jnp.int32, (blk, blk), 1)
    causal = key_i <= qry_i

    exp2_scale = scale * LOG2_E
    for i in range(nb):
        slot = i % 2
        qi = q_ref[i * blk:(i + 1) * blk, :]
        m_run = None
        for jb in range(i + 1):
            kj = k_ref[jb * blk:(jb + 1) * blk, :]
            s = lax.dot_general(kj, qi, nt, preferred_element_type=F32)
            if jb == i:
                s = jnp.where(causal, s, NEG)
            else:
                s = jnp.where(sel[jb:jb + 1, i * blk:(i + 1) * blk] > 0.5, s, NEG)
            s_ref[slot, jb] = s
            m_run = s if m_run is None else jnp.maximum(m_run, s)
        m = jnp.max(m_run, axis=0, keepdims=True)
        l_run = None
        acc = None
        for jb in range(i + 1):
            p = jnp.exp2((s_ref[slot, jb] - m) * exp2_scale)
            l_run = p if l_run is None else l_run + p
            vj = v_ref[jb * blk:(jb + 1) * blk, :]
            pv = lax.dot_general(vj, p.astype(BF16), tn, preferred_element_type=F32)
            acc = pv if acc is None else acc + pv
        l = jnp.sum(l_run, axis=0, keepdims=True)
        o_ref[i * blk:(i + 1) * blk, :] = (acc / l).T.astype(BF16)


def _moba_attention(qkv, ksum, bsz, seq):
    t, d3 = qkv.shape
    d = d3 // 3
    nh = d // HEAD_DIM
    nb = seq // MOBA_BLOCK
    return pl.pallas_call(
        _moba_kernel,
        grid=(bsz, nh),
        in_specs=[
            pl.BlockSpec((seq, HEAD_DIM), lambda b, h: (b, h)),
            pl.BlockSpec((seq, HEAD_DIM), lambda b, h: (b, nh + h)),
            pl.BlockSpec((seq, HEAD_DIM), lambda b, h: (b, 2 * nh + h)),
            pl.BlockSpec((nb, HEAD_DIM), lambda b, h: (b, h)),
        ],
        out_specs=pl.BlockSpec((seq, HEAD_DIM), lambda b, h: (b, h)),
        out_shape=jax.ShapeDtypeStruct((t, d), BF16),
        scratch_shapes=[pltpu.VMEM((2, nb, MOBA_BLOCK, MOBA_BLOCK), F32)],
        compiler_params=_params(("arbitrary", "arbitrary")),
        name="moba_attention",
    )(qkv, qkv, qkv, ksum)


def _oproj_kernel(x_ref, o_ref, w_ref, gpost_ref, out_ref):
    m = jnp.dot(o_ref[...], w_ref[...], preferred_element_type=F32)
    out_ref[...] = x_ref[...] + _rms(m, gpost_ref[...])


def _out_proj(x, o, w_o, g_post, lead):
    t, d = x.shape
    tm = _pick_tile(t, 512, SUBLANES)
    return pl.pallas_call(
        _oproj_kernel,
        grid=(t // tm,),
        in_specs=[
            pl.BlockSpec((tm, d), lambda i: (i, 0)),
            pl.BlockSpec((tm, d), lambda i: (i, 0)),
            _stacked_spec(lead, (d, d), lambda i: (0, 0)),
            pl.BlockSpec((1, d), lambda i: (0, 0)),
        ],
        out_specs=pl.BlockSpec((tm, d), lambda i: (i, 0)),
        out_shape=jax.ShapeDtypeStruct((t, d), F32),
        compiler_params=_params(("arbitrary",)),
        name="attn_out_proj",
    )(x, o, w_o, g_post)


def kernel(x, norm_g, ffn_w_in, ffn_w_out, conv_w_in, conv_w, conv_w_out, attn_w_qkv, attn_w_o):
    bsz, seq, d = x.shape
    depth = norm_g.shape[0]
    assert d % HEAD_DIM == 0 and seq % MOBA_BLOCK == 0
    assert min(MOBA_TOPK, seq // MOBA_BLOCK - 1) == MOBA_TOPK
    h = x.reshape(bsz * seq, d)
    ffn_w_in, ffn_w_out, conv_w_in, conv_w_out, attn_w_qkv, attn_w_o = (
        w.astype(BF16) for w in (ffn_w_in, ffn_w_out, conv_w_in, conv_w_out, attn_w_qkv, attn_w_o))
    for i in range(depth):
        g = norm_g[i][:, None, :]
        h = _ffn(h, g[0], g[1], ffn_w_in, ffn_w_out, (i, 0))
        jm = i // 2
        if i % 2 == 0:
            h = _conv_mixer(h, seq, g[2], g[3], conv_w_in, conv_w, conv_w_out, (jm,))
        else:
            qkv, ksum = _qkv_proj(h, seq, g[2], attn_w_qkv, (jm,))
            o = _moba_attention(qkv, ksum, bsz, seq)
            h = _out_proj(h, o, attn_w_o, g[3], (jm,))
        h = _ffn(h, g[4], g[5], ffn_w_in, ffn_w_out, (i, 1))
    return h.reshape(bsz, seq, d)
```

```python
import functools

import jax
import jax.numpy as jnp
from jax import lax
from jax.experimental import pallas as pl
from jax.experimental.pallas import tpu as pltpu

HEAD_DIM = 128
ROT_DIM = HEAD_DIM // 4
ROPE_THETA = 500000.0
MOBA_BLOCK = 256
MOBA_TOPK = 3
CONV_WIDTH = 3
RMS_EPS = 1e-6
NEG = -1e30
LOG2_E = 1.4426950408889634

LANES = 128
SUBLANES = 8
V7X_VMEM_LIMIT_BYTES = 56 * 1024 * 1024

F32 = jnp.float32
BF16 = jnp.bfloat16


def _pick_tile(n, target, multiple):
    t = min(n, target)
    t -= t % multiple
    while t > multiple and n % t:
        t -= multiple
    assert t > 0 and n % t == 0, (n, target, multiple)
    return t


def _rms(x, g):
    ms = jnp.mean(x * x, axis=-1, keepdims=True)
    return (x * lax.rsqrt(ms + RMS_EPS)) * g


def _params(sem):
    return pltpu.CompilerParams(dimension_semantics=sem,
                                vmem_limit_bytes=V7X_VMEM_LIMIT_BYTES)


def _stacked_spec(lead, block, index_fn):
    lead = tuple(lead)
    return pl.BlockSpec((None,) * len(lead) + tuple(block), lambda *g: lead + tuple(index_fn(*g)))


BF16_SUBLANES = 16


def _slices_per_tile(tm, nj):
    n = min(nj, tm // BF16_SUBLANES)
    while tm % n or (tm // n) % BF16_SUBLANES:
        n -= 1
    return n


def _dependent_zero(*vals):
    words = []
    for v in vals:
        bits = pltpu.bitcast(v, jnp.uint32)
        words += [bits[s:s + SUBLANES, c:c + LANES]
                  for c in range(0, bits.shape[1], LANES) for s in range(0, bits.shape[0], SUBLANES)]
    while len(words) > 1:
        words = [a | b for a, b in zip(words[0::2], words[1::2])] + words[len(words) & ~1:]
    return (words[0] >> 16) >> 16


def _three_stage_step(n_tiles, n_slices, res_scale, down_lhs, xn_ref, xf_ref, gpre_ref, gpost_ref,
                      wo_ref, o_ref, h_ref, acc_ref):
    r = pl.program_id(0)
    j = pl.program_id(1)
    sr = xn_ref.shape[0]
    rows = pl.ds(pl.multiple_of(jnp.minimum(j, n_slices - 1) * sr, sr), sr)
    norm_slot = r % 2
    mm_slot = 1 - norm_slot

    @pl.when((r == 0) & (j == 0))
    def _():
        acc_ref[...] = jnp.zeros_like(acc_ref)

    def finalize_slice():
        out = xf_ref[...] + res_scale * _rms(acc_ref[norm_slot, rows, :], gpost_ref[...])
        o_ref[...] = out
        return out

    def norm_slice():
        hn = _rms(xn_ref[...], gpre_ref[...])
        h_ref[norm_slot, rows, :] = hn.astype(BF16)
        return hn

    def anchor(v):
        zero = _dependent_zero(finalize_slice(), norm_slice())
        zero = jnp.tile(zero, (v.shape[0] // SUBLANES, v.shape[1] // LANES))
        return pltpu.bitcast(pltpu.bitcast(v, jnp.uint32) | zero, F32)

    has_matmul = (r >= 1) & (r <= n_tiles)

    @pl.when(has_matmul)
    def _():
        a = down_lhs(h_ref[mm_slot], anchor, r - 1)
        acc_ref[mm_slot] = (jnp.dot(a, wo_ref[...], preferred_element_type=F32)
                            + jnp.where(j == 0, 0.0, acc_ref[mm_slot]))

    @pl.when(jnp.logical_not(has_matmul))
    def _():
        finalize_slice()
        norm_slice()


def _three_stage_call(name, body, x, g_pre, g_post, tm, nj, chunked_inputs, scratch_shapes):
    t, d = x.shape
    n_tiles = t // tm
    n_slices = _slices_per_tile(tm, nj)
    sr = tm // n_slices

    def norm_slice(r, j):
        return (jnp.minimum(r, n_tiles - 1) * n_slices + jnp.minimum(j, n_slices - 1), 0)

    def final_slice(r, j):
        return (jnp.where(r < 2, 0, (r - 2) * n_slices + jnp.minimum(j, n_slices - 1)), 0)

    def chunk(r, j):
        return jnp.where((r >= 1) & (r <= n_tiles), j, 0)

    return pl.pallas_call(
        functools.partial(body, n_tiles, n_slices),
        grid=(n_tiles + 2, nj),
        in_specs=[
            pl.BlockSpec((sr, d), norm_slice),
            pl.BlockSpec((sr, d), final_slice),
            pl.BlockSpec((1, d), lambda r, j: (0, 0)),
            pl.BlockSpec((1, d), lambda r, j: (0, 0)),
        ] + [spec_fn(chunk) for _, spec_fn in chunked_inputs],
        out_specs=pl.BlockSpec((sr, d), final_slice),
        out_shape=jax.ShapeDtypeStruct((t, d), F32),
        scratch_shapes=[pltpu.VMEM((2, tm, d), BF16), pltpu.VMEM((2, tm, d), F32)] + scratch_shapes,
        compiler_params=_params(("arbitrary", "arbitrary")),
        name=name,
    )(x, x, g_pre, g_post, *[a for a, _ in chunked_inputs])


def _ffn_kernel(n_tiles, n_slices, xn_ref, xf_ref, gpre_ref, gpost_ref, wg_ref, wu_ref, wo_ref,
                o_ref, h_ref, acc_ref):
    def down_lhs(h, anchor, tile):
        g = jnp.dot(h, wg_ref[...], preferred_element_type=F32)
        u = anchor(jnp.dot(h, wu_ref[...], preferred_element_type=F32))
        return (jax.nn.silu(g) * u).astype(BF16)

    _three_stage_step(n_tiles, n_slices, 0.5, down_lhs, xn_ref, xf_ref, gpre_ref, gpost_ref,
                      wo_ref, o_ref, h_ref, acc_ref)


def _ffn(x, g_pre, g_post, w_in, w_out, lead):
    t, d = x.shape
    f = w_out.shape[-2]
    tm = _pick_tile(t, 1024, BF16_SUBLANES)
    tf = _pick_tile(f, 512, LANES)
    nj = f // tf
    return _three_stage_call(
        "ffn_half_step", _ffn_kernel, x, g_pre, g_post, tm, nj,
        [(w_in, lambda chunk: _stacked_spec(lead, (d, tf), lambda r, j: (0, chunk(r, j)))),
         (w_in, lambda chunk: _stacked_spec(lead, (d, tf), lambda r, j: (0, nj + chunk(r, j)))),
         (w_out, lambda chunk: _stacked_spec(lead, (tf, d), lambda r, j: (chunk(r, j), 0)))],
        [])


def _conv_kernel(tiles_per_seq, n_tiles, n_slices, xn_ref, xf_ref, gpre_ref, gpost_ref, wb_ref,
                 wc_ref, wu_ref, cw_ref, wo_ref, o_ref, h_ref, acc_ref, zbuf_ref, halo_ref):
    j = pl.program_id(1)
    tm = h_ref.shape[1]

    @pl.when((pl.program_id(0) == 0) & (j == 0))
    def _():
        halo_ref[...] = jnp.zeros_like(halo_ref)

    def down_lhs(h, anchor, tile):
        b = jnp.dot(h, wb_ref[...], preferred_element_type=F32)
        c = jnp.dot(h, wc_ref[...], preferred_element_type=F32)
        u = anchor(jnp.dot(h, wu_ref[...], preferred_element_type=F32))
        z = c * u

        zbuf_ref[pl.ds(SUBLANES, tm), :] = z
        prev = halo_ref[j]
        zbuf_ref[pl.ds(0, SUBLANES), :] = jnp.where(tile % tiles_per_seq == 0,
                                                    jnp.zeros_like(prev), prev)
        halo_ref[j] = z[tm - SUBLANES:, :]

        cw = cw_ref[...]
        zc = (cw[0:1, :] * zbuf_ref[pl.ds(SUBLANES - 2, tm), :]
              + cw[1:2, :] * zbuf_ref[pl.ds(SUBLANES - 1, tm), :]
              + cw[2:3, :] * z)
        return (b * zc).astype(BF16)

    _three_stage_step(n_tiles, n_slices, 1.0, down_lhs, xn_ref, xf_ref, gpre_ref, gpost_ref,
                      wo_ref, o_ref, h_ref, acc_ref)


def _conv_mixer(x, seq, g_pre, g_post, w_in, conv_w, w_out, lead):
    t, d = x.shape
    tm = _pick_tile(seq, 512, BF16_SUBLANES)
    tn = _pick_tile(d, 512, LANES)
    nj = d // tn

    def w_in_cols(part):
        return lambda chunk: _stacked_spec(lead, (d, tn), lambda r, j: (0, part * nj + chunk(r, j)))

    return _three_stage_call(
        "short_conv_mixer", functools.partial(_conv_kernel, seq // tm), x, g_pre, g_post, tm, nj,
        [(w_in, w_in_cols(0)), (w_in, w_in_cols(1)), (w_in, w_in_cols(2)),
         (conv_w, lambda chunk: _stacked_spec(lead, (CONV_WIDTH, tn), lambda r, j: (0, chunk(r, j)))),
         (w_out, lambda chunk: _stacked_spec(lead, (tn, d), lambda r, j: (chunk(r, j), 0)))],
        [pltpu.VMEM((tm + SUBLANES, tn), F32), pltpu.VMEM((nj, SUBLANES, tn), F32)])


MXU_WIDTH = 256


def _qkv_pass_col(j):
    return (2 * j) % 3


def _qkv_kernel(x_ref, gpre_ref, w_ref, cos_ref, sin_ref, o_ref, ksum_ref, h_ref):
    j = pl.program_id(1)
    tm, d = x_ref.shape
    half = ROT_DIM // 2
    nb = tm // MOBA_BLOCK

    @pl.when(j == 0)
    def _():
        h_ref[...] = _rms(x_ref[...], gpre_ref[...]).astype(BF16)

    rotary = j != 1
    cos = jnp.where(rotary, cos_ref[...], 1.0)
    sin = jnp.where(rotary, sin_ref[...], 0.0)
    lane = lax.broadcasted_iota(jnp.int32, (tm, HEAD_DIM), 1)
    pad_rows = jnp.zeros((SUBLANES - nb, HEAD_DIM), F32)
    h = h_ref[...]
    for c in range(d // MXU_WIDTH):
        r = jnp.dot(h, w_ref[:, c * MXU_WIDTH:(c + 1) * MXU_WIDTH], preferred_element_type=F32)
        for hh in range(MXU_WIDTH // HEAD_DIM):
            th = r[:, hh * HEAD_DIM:(hh + 1) * HEAD_DIM]
            partner = jnp.where(lane < half,
                                pltpu.roll(th, HEAD_DIM - half, 1),
                                pltpu.roll(th, half, 1))
            rot = th * cos + partner * sin
            cols = slice(c * MXU_WIDTH + hh * HEAD_DIM, c * MXU_WIDTH + (hh + 1) * HEAD_DIM)
            o_ref[:, cols] = rot.astype(BF16)
            sums = jnp.sum(rot.reshape(nb, MOBA_BLOCK, HEAD_DIM), axis=1)
            ksum_ref[0, :, cols] = jnp.concatenate([sums, pad_rows], axis=0) if nb < SUBLANES else sums


def _rope_tables(seq):
    half = ROT_DIM // 2
    inv_freq = ROPE_THETA ** (-jnp.arange(0, ROT_DIM, 2, dtype=F32) / ROT_DIM)
    ang = jnp.arange(seq, dtype=F32)[:, None] * inv_freq[None, :]
    cos, sin = jnp.cos(ang), jnp.sin(ang)
    pad = HEAD_DIM - ROT_DIM
    cos_t = jnp.concatenate([cos, cos, jnp.ones((seq, pad), F32)], axis=-1)
    sin_t = jnp.concatenate([-sin, sin, jnp.zeros((seq, pad), F32)], axis=-1)
    return cos_t, sin_t


def _qkv_proj(x, seq, g_pre, w_qkv, lead):
    t, d = x.shape
    tm = _pick_tile(seq, 512, MOBA_BLOCK)
    assert tm // MOBA_BLOCK <= SUBLANES
    tiles_per_seq = seq // tm
    cos_t, sin_t = _rope_tables(seq)
    qkv, ksum = pl.pallas_call(
        _qkv_kernel,
        grid=(t // tm, 3),
        in_specs=[
            pl.BlockSpec((tm, d), lambda i, j: (i, 0)),
            pl.BlockSpec((1, d), lambda i, j: (0, 0)),
            _stacked_spec(lead, (d, d), lambda i, j: (0, _qkv_pass_col(j))),
            pl.BlockSpec((tm, HEAD_DIM), lambda i, j: (i % tiles_per_seq, 0)),
            pl.BlockSpec((tm, HEAD_DIM), lambda i, j: (i % tiles_per_seq, 0)),
        ],
        out_specs=[
            pl.BlockSpec((tm, d), lambda i, j: (i, _qkv_pass_col(j))),
            pl.BlockSpec((1, SUBLANES, d), lambda i, j: (i, 0, 0)),
        ],
        out_shape=[
            jax.ShapeDtypeStruct((t, 3 * d), BF16),
            jax.ShapeDtypeStruct((t // tm, SUBLANES, d), F32),
        ],
        scratch_shapes=[pltpu.VMEM((tm, d), BF16)],
        compiler_params=_params(("arbitrary", "arbitrary")),
        name="qkv_rope_proj",
    )(x, g_pre, w_qkv, cos_t, sin_t)
    ksum = ksum[:, :tm // MOBA_BLOCK, :].reshape(t // MOBA_BLOCK, d)
    return qkv, ksum


def _moba_kernel(q_ref, k_ref, v_ref, ksum_ref, o_ref, s_ref):
    seq = q_ref.shape[0]
    nb = seq // MOBA_BLOCK
    blk = MOBA_BLOCK
    scale = HEAD_DIM ** -0.5
    nt = (((1,), (1,)), ((), ()))
    tn = (((0,), (0,)), ((), ()))

    kmean = (ksum_ref[...] * (1.0 / blk)).astype(BF16)
    gate = lax.dot_general(kmean, q_ref[...], nt, preferred_element_type=F32)
    n_idx = lax.broadcasted_iota(jnp.int32, (nb, seq), 0)
    q_blk = lax.broadcasted_iota(jnp.int32, (nb, seq), 1) // blk
    rank = jnp.zeros((nb, seq), jnp.int32)
    for m in range(nb - 1):
        gm = gate[m:m + 1, :]
        beats = (gm > gate) | ((gm == gate) & (m < n_idx))
        rank = rank + jnp.where(beats & (m < q_blk), 1, 0)
    sel = jnp.where((n_idx < q_blk) & (rank < MOBA_TOPK), 1.0, 0.0).astype(F32)

    key_i = lax.broadcasted_iota(jnp.int32, (blk, blk), 0)
    qry_i = lax.broadcasted_iota(jnp.int32, (blk, blk), 1)
    causal = key_i <= qry_i

    exp2_scale = scale * LOG2_E
    for i in range(nb):
        slot = i % 2
        qi = q_ref[i * blk:(i + 1) * blk, :]
        m_run = None
        for jb in range(i + 1):
            kj = k_ref[jb * blk:(jb + 1) * blk, :]
            s = lax.dot_general(kj, qi, nt, preferred_element_type=F32)
            if jb == i:
                s = jnp.where(causal, s, NEG)
            else:
                s = jnp.where(sel[jb:jb + 1, i * blk:(i + 1) * blk] > 0.5, s, NEG)
            s_ref[slot, jb] = s
            m_run = s if m_run is None else jnp.maximum(m_run, s)
        m = jnp.max(m_run, axis=0, keepdims=True)
        l_run = None
        acc = None
        for jb in range(i + 1):
            p = jnp.exp2((s_ref[slot, jb] - m) * exp2_scale)
            l_run = p if l_run is None else l_run + p
            vj = v_ref[jb * blk:(jb + 1) * blk, :]
            pv = lax.dot_general(vj, p.astype(BF16), tn, preferred_element_type=F32)
            acc = pv if acc is None else acc + pv
        l = jnp.sum(l_run, axis=0, keepdims=True)
        o_ref[i * blk:(i + 1) * blk, :] = (acc / l).T.astype(BF16)


def _moba_attention(qkv, ksum, bsz, seq):
    t, d3 = qkv.shape
    d = d3 // 3
    nh = d // HEAD_DIM
    nb = seq // MOBA_BLOCK
    return pl.pallas_call(
        _moba_kernel,
        grid=(bsz, nh),
        in_specs=[
            pl.BlockSpec((seq, HEAD_DIM), lambda b, h: (b, h)),
            pl.BlockSpec((seq, HEAD_DIM), lambda b, h: (b, nh + h)),
            pl.BlockSpec((seq, HEAD_DIM), lambda b, h: (b, 2 * nh + h)),
            pl.BlockSpec((nb, HEAD_DIM), lambda b, h: (b, h)),
        ],
        out_specs=pl.BlockSpec((seq, HEAD_DIM), lambda b, h: (b, h)),
        out_shape=jax.ShapeDtypeStruct((t, d), BF16),
        scratch_shapes=[pltpu.VMEM((2, nb, MOBA_BLOCK, MOBA_BLOCK), F32)],
        compiler_params=_params(("arbitrary", "arbitrary")),
        name="moba_attention",
    )(qkv, qkv, qkv, ksum)


def _oproj_kernel(x_ref, o_ref, w_ref, gpost_ref, out_ref):
    m = jnp.dot(o_ref[...], w_ref[...], preferred_element_type=F32)
    out_ref[...] = x_ref[...] + _rms(m, gpost_ref[...])


def _out_proj(x, o, w_o, g_post, lead):
    t, d = x.shape
    tm = _pick_tile(t, 512, SUBLANES)
    return pl.pallas_call(
        _oproj_kernel,
        grid=(t // tm,),
        in_specs=[
            pl.BlockSpec((tm, d), lambda i: (i, 0)),
            pl.BlockSpec((tm, d), lambda i: (i, 0)),
            _stacked_spec(lead, (d, d), lambda i: (0, 0)),
            pl.BlockSpec((1, d), lambda i: (0, 0)),
        ],
        out_specs=pl.BlockSpec((tm, d), lambda i: (i, 0)),
        out_shape=jax.ShapeDtypeStruct((t, d), F32),
        compiler_params=_params(("arbitrary",)),
        name="attn_out_proj",
    )(x, o, w_o, g_post)


def kernel(x, norm_g, ffn_w_in, ffn_w_out, conv_w_in, conv_w, conv_w_out, attn_w_qkv, attn_w_o):
    bsz, seq, d = x.shape
    depth = norm_g.shape[0]
    assert d % HEAD_DIM == 0 and seq % MOBA_BLOCK == 0
    assert min(MOBA_TOPK, seq // MOBA_BLOCK - 1) == MOBA_TOPK
    h = x.reshape(bsz * seq, d)
    ffn_w_in, ffn_w_out, conv_w_in, conv_w_out, attn_w_qkv, attn_w_o = (
        w.astype(BF16) for w in (ffn_w_in, ffn_w_out, conv_w_in, conv_w_out, attn_w_qkv, attn_w_o))
    for i in range(depth):
        g = norm_g[i][:, None, :]
        h = _ffn(h, g[0], g[1], ffn_w_in, ffn_w_out, (i, 0))
        jm = i // 2
        if i % 2 == 0:
            h = _conv_mixer(h, seq, g[2], g[3], conv_w_in, conv_w, conv_w_out, (jm,))
        else:
            qkv, ksum = _qkv_proj(h, seq, g[2], attn_w_qkv, (jm,))
            o = _moba_attention(qkv, ksum, bsz, seq)
            h = _out_proj(h, o, attn_w_o, g[3], (jm,))
        h = _ffn(h, g[4], g[5], ffn_w_in, ffn_w_out, (i, 1))
    return h.reshape(bsz, seq, d)
```

```python
import functools

import jax
import jax.numpy as jnp
from jax import lax
from jax.experimental import pallas as pl
from jax.experimental.pallas import tpu as pltpu

HEAD_DIM = 128
ROT_DIM = HEAD_DIM // 4
ROPE_THETA = 500000.0
MOBA_BLOCK = 256
MOBA_TOPK = 3
CONV_WIDTH = 3
RMS_EPS = 1e-6
NEG = -1e30
LOG2_E = 1.4426950408889634

LANES = 128
SUBLANES = 8
V7X_VMEM_LIMIT_BYTES = 56 * 1024 * 1024

F32 = jnp.float32
BF16 = jnp.bfloat16


def _pick_tile(n, target, multiple):
    t = min(n, target)
    t -= t % multiple
    while t > multiple and n % t:
        t -= multiple
    assert t > 0 and n % t == 0, (n, target, multiple)
    return t


def _rms(x, g):
    ms = jnp.mean(x * x, axis=-1, keepdims=True)
    return (x * lax.rsqrt(ms + RMS_EPS)) * g


def _params(sem):
    return pltpu.CompilerParams(dimension_semantics=sem,
                                vmem_limit_bytes=V7X_VMEM_LIMIT_BYTES)


def _stacked_spec(lead, block, index_fn):
    lead = tuple(lead)
    return pl.BlockSpec((None,) * len(lead) + tuple(block), lambda *g: lead + tuple(index_fn(*g)))


BF16_SUBLANES = 16


def _slices_per_tile(tm, nj):
    n = min(nj, tm // BF16_SUBLANES)
    while tm % n or (tm // n) % BF16_SUBLANES:
        n -= 1
    return n


def _dependent_zero(*vals):
    words = []
    for v in vals:
        bits = pltpu.bitcast(v, jnp.uint32)
        words += [bits[s:s + SUBLANES, c:c + LANES]
                  for c in range(0, bits.shape[1], LANES) for s in range(0, bits.shape[0], SUBLANES)]
    while len(words) > 1:
        words = [a | b for a, b in zip(words[0::2], words[1::2])] + words[len(words) & ~1:]
    return (words[0] >> 16) >> 16


def _three_stage_step(n_tiles, n_slices, res_scale, down_lhs, xn_ref, xf_ref, gpre_ref, gpost_ref,
                      wo_ref, o_ref, h_ref, acc_ref):
    r = pl.program_id(0)
    j = pl.program_id(1)
    sr = xn_ref.shape[0]
    rows = pl.ds(pl.multiple_of(jnp.minimum(j, n_slices - 1) * sr, sr), sr)
    norm_slot = r % 2
    mm_slot = 1 - norm_slot

    @pl.when((r == 0) & (j == 0))
    def _():
        acc_ref[...] = jnp.zeros_like(acc_ref)

    def finalize_slice():
        out = xf_ref[...] + res_scale * _rms(acc_ref[norm_slot, rows, :], gpost_ref[...])
        o_ref[...] = out
        return out

    def norm_slice():
        hn = _rms(xn_ref[...], gpre_ref[...])
        h_ref[norm_slot, rows, :] = hn.astype(BF16)
        return hn

    def anchor(v):
        zero = _dependent_zero(finalize_slice(), norm_slice())
        zero = jnp.tile(zero, (v.shape[0] // SUBLANES, v.shape[1] // LANES))
        return pltpu.bitcast(pltpu.bitcast(v, jnp.uint32) | zero, F32)

    has_matmul = (r >= 1) & (r <= n_tiles)

    @pl.when(has_matmul)
    def _():
        a = down_lhs(h_ref[mm_slot], anchor, r - 1)
        acc_ref[mm_slot] = (jnp.dot(a, wo_ref[...], preferred_element_type=F32)
                            + jnp.where(j == 0, 0.0, acc_ref[mm_slot]))

    @pl.when(jnp.logical_not(has_matmul))
    def _():
        finalize_slice()
        norm_slice()


def _cast_piece_rows(rows, steps):
    for pr in range(BF16_SUBLANES, rows + 1, BF16_SUBLANES):
        if rows % pr == 0 and rows // pr <= steps:
            return pr
    raise ValueError((rows, steps))


def _cast_jobs(cast_next, steps, step_index):
    arrays, in_specs, out_specs, out_shapes = [], [], [], []
    for w, lead in cast_next:
        rows, cols = w.shape[-2:]
        pr = _cast_piece_rows(rows, steps)

        def piece(*g, last=rows // pr - 1):
            return (jnp.minimum(step_index(*g), last), 0)

        arrays.append(w)
        in_specs.append(_stacked_spec(lead, (pr, cols), piece))
        out_specs.append(pl.BlockSpec((pr, cols), piece))
        out_shapes.append(jax.ShapeDtypeStruct((rows, cols), BF16))
    return arrays, in_specs, out_specs, out_shapes


def _with_cast_jobs(body, n_in, n_cast, *refs):
    for src, dst in zip(refs[n_in:n_in + n_cast], refs[n_in + n_cast + 1:n_in + 2 * n_cast + 1]):
        dst[...] = src[...].astype(BF16)
    body(*refs[:n_in], refs[n_in + n_cast], *refs[n_in + 2 * n_cast + 1:])


def _three_stage_call(name, body, x, g_pre, g_post, tm, nj, chunked_inputs, scratch_shapes,
                      cast_next):
    t, d = x.shape
    n_tiles = t // tm
    n_slices = _slices_per_tile(tm, nj)
    sr = tm // n_slices
    cast_arrays, cast_in, cast_out, cast_shapes = _cast_jobs(
        cast_next, (n_tiles + 2) * nj, lambda r, j: r * nj + j)

    def norm_slice(r, j):
        return (jnp.minimum(r, n_tiles - 1) * n_slices + jnp.minimum(j, n_slices - 1), 0)

    def final_slice(r, j):
        return (jnp.where(r < 2, 0, (r - 2) * n_slices + jnp.minimum(j, n_slices - 1)), 0)

    def chunk(r, j):
        return jnp.where((r >= 1) & (r <= n_tiles), j, 0)

    out, *cast = pl.pallas_call(
        functools.partial(_with_cast_jobs, functools.partial(body, n_tiles, n_slices),
                          4 + len(chunked_inputs), len(cast_arrays)),
        grid=(n_tiles + 2, nj),
        in_specs=[
            pl.BlockSpec((sr, d), norm_slice),
            pl.BlockSpec((sr, d), final_slice),
            pl.BlockSpec((1, d), lambda r, j: (0, 0)),
            pl.BlockSpec((1, d), lambda r, j: (0, 0)),
        ] + [spec_fn(chunk) for _, spec_fn in chunked_inputs] + cast_in,
        out_specs=[pl.BlockSpec((sr, d), final_slice)] + cast_out,
        out_shape=[jax.ShapeDtypeStruct((t, d), F32)] + cast_shapes,
        scratch_shapes=[pltpu.VMEM((2, tm, d), BF16), pltpu.VMEM((2, tm, d), F32)] + scratch_shapes,
        compiler_params=_params(("arbitrary", "arbitrary")),
        name=name,
    )(x, x, g_pre, g_post, *[a for a, _ in chunked_inputs], *cast_arrays)
    return out, cast


def _ffn_kernel(n_tiles, n_slices, xn_ref, xf_ref, gpre_ref, gpost_ref, wg_ref, wu_ref, wo_ref,
                o_ref, h_ref, acc_ref):
    def down_lhs(h, anchor, tile):
        g = jnp.dot(h, wg_ref[...], preferred_element_type=F32)
        u = anchor(jnp.dot(h, wu_ref[...], preferred_element_type=F32))
        return (jax.nn.silu(g) * u).astype(BF16)

    _three_stage_step(n_tiles, n_slices, 0.5, down_lhs, xn_ref, xf_ref, gpre_ref, gpost_ref,
                      wo_ref, o_ref, h_ref, acc_ref)


def _ffn(x, g_pre, g_post, w_in, w_out, cast_next):
    lead = ()
    t, d = x.shape
    f = w_out.shape[-2]
    tm = _pick_tile(t, 1024, BF16_SUBLANES)
    tf = _pick_tile(f, 512, LANES)
    nj = f // tf
    return _three_stage_call(
        "ffn_half_step", _ffn_kernel, x, g_pre, g_post, tm, nj,
        [(w_in, lambda chunk: _stacked_spec(lead, (d, tf), lambda r, j: (0, chunk(r, j)))),
         (w_in, lambda chunk: _stacked_spec(lead, (d, tf), lambda r, j: (0, nj + chunk(r, j)))),
         (w_out, lambda chunk: _stacked_spec(lead, (tf, d), lambda r, j: (chunk(r, j), 0)))],
        [], cast_next)


def _conv_kernel(tiles_per_seq, n_tiles, n_slices, xn_ref, xf_ref, gpre_ref, gpost_ref, wb_ref,
                 wc_ref, wu_ref, cw_ref, wo_ref, o_ref, h_ref, acc_ref, zbuf_ref, halo_ref):
    j = pl.program_id(1)
    tm = h_ref.shape[1]

    @pl.when((pl.program_id(0) == 0) & (j == 0))
    def _():
        halo_ref[...] = jnp.zeros_like(halo_ref)

    def down_lhs(h, anchor, tile):
        b = jnp.dot(h, wb_ref[...], preferred_element_type=F32)
        c = jnp.dot(h, wc_ref[...], preferred_element_type=F32)
        u = anchor(jnp.dot(h, wu_ref[...], preferred_element_type=F32))
        z = c * u

        zbuf_ref[pl.ds(SUBLANES, tm), :] = z
        prev = halo_ref[j]
        zbuf_ref[pl.ds(0, SUBLANES), :] = jnp.where(tile % tiles_per_seq == 0,
                                                    jnp.zeros_like(prev), prev)
        halo_ref[j] = z[tm - SUBLANES:, :]

        cw = cw_ref[...]
        zc = (cw[0:1, :] * zbuf_ref[pl.ds(SUBLANES - 2, tm), :]
              + cw[1:2, :] * zbuf_ref[pl.ds(SUBLANES - 1, tm), :]
              + cw[2:3, :] * z)
        return (b * zc).astype(BF16)

    _three_stage_step(n_tiles, n_slices, 1.0, down_lhs, xn_ref, xf_ref, gpre_ref, gpost_ref,
                      wo_ref, o_ref, h_ref, acc_ref)


def _conv_mixer(x, seq, g_pre, g_post, w_in, conv_w, w_out, cast_next):
    lead = ()
    t, d = x.shape
    tm = _pick_tile(seq, 512, BF16_SUBLANES)
    tn = _pick_tile(d, 512, LANES)
    nj = d // tn

    def w_in_cols(part):
        return lambda chunk: _stacked_spec(lead, (d, tn), lambda r, j: (0, part * nj + chunk(r, j)))

    return _three_stage_call(
        "short_conv_mixer", functools.partial(_conv_kernel, seq // tm), x, g_pre, g_post, tm, nj,
        [(w_in, w_in_cols(0)), (w_in, w_in_cols(1)), (w_in, w_in_cols(2)),
         (conv_w, lambda chunk: _stacked_spec(lead, (CONV_WIDTH, tn), lambda r, j: (0, chunk(r, j)))),
         (w_out, lambda chunk: _stacked_spec(lead, (tn, d), lambda r, j: (chunk(r, j), 0)))],
        [pltpu.VMEM((tm + SUBLANES, tn), F32), pltpu.VMEM((nj, SUBLANES, tn), F32)], cast_next)


MXU_WIDTH = 256


def _qkv_pass_col(j):
    return (2 * j) % 3


def _qkv_kernel(x_ref, gpre_ref, w_ref, cos_ref, sin_ref, o_ref, ksum_ref, h_ref):
    j = pl.program_id(1)
    tm, d = x_ref.shape
    half = ROT_DIM // 2
    nb = tm // MOBA_BLOCK

    @pl.when(j == 0)
    def _():
        h_ref[...] = _rms(x_ref[...], gpre_ref[...]).astype(BF16)

    rotary = j != 1
    cos = jnp.where(rotary, cos_ref[...], 1.0)
    sin = jnp.where(rotary, sin_ref[...], 0.0)
    lane = lax.broadcasted_iota(jnp.int32, (tm, HEAD_DIM), 1)
    pad_rows = jnp.zeros((SUBLANES - nb, HEAD_DIM), F32)
    h = h_ref[...]
    for c in range(d // MXU_WIDTH):
        r = jnp.dot(h, w_ref[:, c * MXU_WIDTH:(c + 1) * MXU_WIDTH], preferred_element_type=F32)
        for hh in range(MXU_WIDTH // HEAD_DIM):
            th = r[:, hh * HEAD_DIM:(hh + 1) * HEAD_DIM]
            partner = jnp.where(lane < half,
                                pltpu.roll(th, HEAD_DIM - half, 1),
                                pltpu.roll(th, half, 1))
            rot = th * cos + partner * sin
            cols = slice(c * MXU_WIDTH + hh * HEAD_DIM, c * MXU_WIDTH + (hh + 1) * HEAD_DIM)
            o_ref[:, cols] = rot.astype(BF16)
            sums = jnp.sum(rot.reshape(nb, MOBA_BLOCK, HEAD_DIM), axis=1)
            ksum_ref[0, :, cols] = jnp.concatenate([sums, pad_rows], axis=0) if nb < SUBLANES else sums


def _rope_tables(seq):
    half = ROT_DIM // 2
    inv_freq = ROPE_THETA ** (-jnp.arange(0, ROT_DIM, 2, dtype=F32) / ROT_DIM)
    ang = jnp.arange(seq, dtype=F32)[:, None] * inv_freq[None, :]
    cos, sin = jnp.cos(ang), jnp.sin(ang)
    pad = HEAD_DIM - ROT_DIM
    cos_t = jnp.concatenate([cos, cos, jnp.ones((seq, pad), F32)], axis=-1)
    sin_t = jnp.concatenate([-sin, sin, jnp.zeros((seq, pad), F32)], axis=-1)
    return cos_t, sin_t


def _qkv_proj(x, seq, g_pre, w_qkv):
    lead = ()
    t, d = x.shape
    tm = _pick_tile(seq, 512, MOBA_BLOCK)
    assert tm // MOBA_BLOCK <= SUBLANES
    tiles_per_seq = seq // tm
    cos_t, sin_t = _rope_tables(seq)
    qkv, ksum = pl.pallas_call(
        _qkv_kernel,
        grid=(t // tm, 3),
        in_specs=[
            pl.BlockSpec((tm, d), lambda i, j: (i, 0)),
            pl.BlockSpec((1, d), lambda i, j: (0, 0)),
            _stacked_spec(lead, (d, d), lambda i, j: (0, _qkv_pass_col(j))),
            pl.BlockSpec((tm, HEAD_DIM), lambda i, j: (i % tiles_per_seq, 0)),
            pl.BlockSpec((tm, HEAD_DIM), lambda i, j: (i % tiles_per_seq, 0)),
        ],
        out_specs=[
            pl.BlockSpec((tm, d), lambda i, j: (i, _qkv_pass_col(j))),
            pl.BlockSpec((1, SUBLANES, d), lambda i, j: (i, 0, 0)),
        ],
        out_shape=[
            jax.ShapeDtypeStruct((t, 3 * d), BF16),
            jax.ShapeDtypeStruct((t // tm, SUBLANES, d), F32),
        ],
        scratch_shapes=[pltpu.VMEM((tm, d), BF16)],
        compiler_params=_params(("arbitrary", "arbitrary")),
        name="qkv_rope_proj",
    )(x, g_pre, w_qkv, cos_t, sin_t)
    ksum = ksum[:, :tm // MOBA_BLOCK, :].reshape(t // MOBA_BLOCK, d)
    return qkv, ksum


MOBA_HEADS_PER_STEP = 2


def _moba_kernel(q_ref, k_ref, v_ref, ksum_ref, o_ref, s_ref):
    for hh in range(q_ref.shape[1] // HEAD_DIM):
        cols = pl.ds(hh * HEAD_DIM, HEAD_DIM)
        _moba_head(q_ref.at[:, cols], k_ref.at[:, cols], v_ref.at[:, cols], ksum_ref.at[:, cols],
                   o_ref.at[:, cols], s_ref.at[hh])


def _moba_head(q_ref, k_ref, v_ref, ksum_ref, o_ref, s_ref):
    seq = q_ref.shape[0]
    nb = seq // MOBA_BLOCK
    blk = MOBA_BLOCK
    scale = HEAD_DIM ** -0.5
    nt = (((1,), (1,)), ((), ()))
    tn = (((0,), (0,)), ((), ()))

    kmean = (ksum_ref[...] * (1.0 / blk)).astype(BF16)
    gate = lax.dot_general(kmean, q_ref[...], nt, preferred_element_type=F32)
    n_idx = lax.broadcasted_iota(jnp.int32, (nb, seq), 0)
    q_blk = lax.broadcasted_iota(jnp.int32, (nb, seq), 1) // blk
    rank = jnp.zeros((nb, seq), jnp.int32)
    for m in range(nb - 1):
        gm = gate[m:m + 1, :]
        beats = (gm > gate) | ((gm == gate) & (m < n_idx))
        rank = rank + jnp.where(beats & (m < q_blk), 1, 0)
    sel = jnp.where((n_idx < q_blk) & (rank < MOBA_TOPK), 1.0, 0.0).astype(F32)

    key_i = lax.broadcasted_iota(jnp.int32, (blk, blk), 0)
    qry_i = lax.broadcasted_iota(jnp.int32, (blk, blk), 1)
    causal = key_i <= qry_i

    exp2_scale = scale * LOG2_E
    for i in range(nb):
        slot = i % 2
        qi = q_ref[i * blk:(i + 1) * blk, :]
        m_run = None
        for jb in range(i + 1):
            kj = k_ref[jb * blk:(jb + 1) * blk, :]
            s = lax.dot_general(kj, qi, nt, preferred_element_type=F32)
            if jb == i:
                s = jnp.where(causal, s, NEG)
            else:
                s = jnp.where(sel[jb:jb + 1, i * blk:(i + 1) * blk] > 0.5, s, NEG)
            s_ref[slot, jb] = s
            m_run = s if m_run is None else jnp.maximum(m_run, s)
        m = jnp.max(m_run, axis=0, keepdims=True)
        l_run = None
        acc = None
        for jb in range(i + 1):
            p = jnp.exp2((s_ref[slot, jb] - m) * exp2_scale)
            l_run = p if l_run is None else l_run + p
            vj = v_ref[jb * blk:(jb + 1) * blk, :]
            pv = lax.dot_general(vj, p.astype(BF16), tn, preferred_element_type=F32)
            acc = pv if acc is None else acc + pv
        l = jnp.sum(l_run, axis=0, keepdims=True)
        o_ref[i * blk:(i + 1) * blk, :] = (acc / l).T.astype(BF16)


def _moba_attention(qkv, ksum, bsz, seq, cast_next):
    t, d3 = qkv.shape
    d = d3 // 3
    hp = MOBA_HEADS_PER_STEP if (d // HEAD_DIM) % MOBA_HEADS_PER_STEP == 0 else 1
    nh = d // (HEAD_DIM * hp)
    nb = seq // MOBA_BLOCK
    width = hp * HEAD_DIM
    cast_arrays, cast_in, cast_out, cast_shapes = _cast_jobs(
        cast_next, bsz * nh, lambda b, h: b * nh + h)
    out, *cast = pl.pallas_call(
        functools.partial(_with_cast_jobs, _moba_kernel, 4, len(cast_arrays)),
        grid=(bsz, nh),
        in_specs=[
            pl.BlockSpec((seq, width), lambda b, h: (b, h)),
            pl.BlockSpec((seq, width), lambda b, h: (b, nh + h)),
            pl.BlockSpec((seq, width), lambda b, h: (b, 2 * nh + h)),
            pl.BlockSpec((nb, width), lambda b, h: (b, h)),
        ] + cast_in,
        out_specs=[pl.BlockSpec((seq, width), lambda b, h: (b, h))] + cast_out,
        out_shape=[jax.ShapeDtypeStruct((t, d), BF16)] + cast_shapes,
        scratch_shapes=[pltpu.VMEM((hp, 2, nb, MOBA_BLOCK, MOBA_BLOCK), F32)],
        compiler_params=_params(("arbitrary", "arbitrary")),
        name="moba_attention",
    )(qkv, qkv, qkv, ksum, *cast_arrays)
    return out, cast


def _oproj_kernel(x_ref, o_ref, w_ref, gpost_ref, out_ref):
    m = jnp.dot(o_ref[...], w_ref[...], preferred_element_type=F32)
    out_ref[...] = x_ref[...] + _rms(m, gpost_ref[...])


def _out_proj(x, o, w_o, g_post):
    lead = ()
    t, d = x.shape
    tm = _pick_tile(t, 512, SUBLANES)
    return pl.pallas_call(
        _oproj_kernel,
        grid=(t // tm,),
        in_specs=[
            pl.BlockSpec((tm, d), lambda i: (i, 0)),
            pl.BlockSpec((tm, d), lambda i: (i, 0)),
            _stacked_spec(lead, (d, d), lambda i: (0, 0)),
            pl.BlockSpec((1, d), lambda i: (0, 0)),
        ],
        out_specs=pl.BlockSpec((tm, d), lambda i: (i, 0)),
        out_shape=jax.ShapeDtypeStruct((t, d), F32),
        compiler_params=_params(("arbitrary",)),
        name="attn_out_proj",
    )(x, o, w_o, g_post)


def kernel(x, norm_g, ffn_w_in, ffn_w_out, conv_w_in, conv_w, conv_w_out, attn_w_qkv, attn_w_o):
    bsz, seq, d = x.shape
    depth = norm_g.shape[0]
    assert d % HEAD_DIM == 0 and seq % MOBA_BLOCK == 0
    assert min(MOBA_TOPK, seq // MOBA_BLOCK - 1) == MOBA_TOPK
    h = x.reshape(bsz * seq, d)

    def ffn_weights(i, half):
        return [(ffn_w_in, (i, half)), (ffn_w_out, (i, half))]

    def mixer_weights(i):
        if i % 2 == 0:
            return [(conv_w_in, (i // 2,)), (conv_w_out, (i // 2,))]
        return [(attn_w_qkv, (i // 2,)), (attn_w_o, (i // 2,))]

    w_in, w_out = ffn_w_in[0, 0].astype(BF16), ffn_w_out[0, 0].astype(BF16)
    for i in range(depth):
        g = norm_g[i][:, None, :]
        h, (wm_a, wm_b) = _ffn(h, g[0], g[1], w_in, w_out, mixer_weights(i))
        if i % 2 == 0:
            h, (w_in, w_out) = _conv_mixer(h, seq, g[2], g[3], wm_a, conv_w[i // 2], wm_b,
                                           ffn_weights(i, 1))
        else:
            qkv, ksum = _qkv_proj(h, seq, g[2], wm_a)
            o, (w_in, w_out) = _moba_attention(qkv, ksum, bsz, seq, ffn_weights(i, 1))
            h = _out_proj(h, o, wm_b, g[3])
        last = i == depth - 1
        h, nxt = _ffn(h, g[4], g[5], w_in, w_out, [] if last else ffn_weights(i + 1, 0))
        if not last:
            w_in, w_out = nxt
    return h.reshape(bsz, seq, d)
```

```python
import functools

import jax
import jax.numpy as jnp
from jax import lax
from jax.experimental import pallas as pl
from jax.experimental.pallas import tpu as pltpu

HEAD_DIM = 128
ROT_DIM = HEAD_DIM // 4
ROPE_THETA = 500000.0
MOBA_BLOCK = 256
MOBA_TOPK = 3
CONV_WIDTH = 3
RMS_EPS = 1e-6
NEG = -1e30
LOG2_E = 1.4426950408889634

LANES = 128
SUBLANES = 8
MXU_WIDTH = 256
V7X_VMEM_LIMIT_BYTES = 56 * 1024 * 1024

F32 = jnp.float32
BF16 = jnp.bfloat16


def _pick_tile(n, target, multiple):
    t = min(n, target)
    t -= t % multiple
    while t > multiple and n % t:
        t -= multiple
    assert t > 0 and n % t == 0, (n, target, multiple)
    return t


def _rms(x, g):
    ms = jnp.mean(x * x, axis=-1, keepdims=True)
    return (x * lax.rsqrt(ms + RMS_EPS)) * g


def _params(sem):
    return pltpu.CompilerParams(dimension_semantics=sem,
                                vmem_limit_bytes=V7X_VMEM_LIMIT_BYTES)


def _stacked_spec(lead, block, index_fn):
    lead = tuple(lead)
    return pl.BlockSpec((None,) * len(lead) + tuple(block), lambda *g: lead + tuple(index_fn(*g)))


BF16_SUBLANES = 16


def _slices_per_tile(tm, nj):
    n = min(nj, tm // BF16_SUBLANES)
    while tm % n or (tm // n) % BF16_SUBLANES:
        n -= 1
    return n


def _dependent_zero(*vals):
    words = []
    for v in vals:
        bits = pltpu.bitcast(v, jnp.uint32)
        words += [bits[s:s + SUBLANES, c:c + LANES]
                  for c in range(0, bits.shape[1], LANES) for s in range(0, bits.shape[0], SUBLANES)]
    while len(words) > 1:
        words = [a | b for a, b in zip(words[0::2], words[1::2])] + words[len(words) & ~1:]
    return (words[0] >> 16) >> 16


def _three_stage_step(n_tiles, n_slices, res_scale, down_lhs, xn_ref, xf_ref, gpre_ref, gpost_ref,
                      wo_ref, o_ref, h_ref, acc_ref):
    r = pl.program_id(0)
    j = pl.program_id(1)
    sr = xn_ref.shape[0]
    rows = pl.ds(pl.multiple_of(jnp.minimum(j, n_slices - 1) * sr, sr), sr)
    norm_slot = r % 2
    mm_slot = 1 - norm_slot

    @pl.when((r == 0) & (j == 0))
    def _():
        acc_ref[...] = jnp.zeros_like(acc_ref)

    def finalize_slice():
        out = xf_ref[...] + res_scale * _rms(acc_ref[norm_slot, rows, :], gpost_ref[...])
        o_ref[...] = out
        return out

    def norm_slice():
        hn = _rms(xn_ref[...], gpre_ref[...])
        h_ref[norm_slot, rows, :] = hn.astype(BF16)
        return hn

    def anchor(v):
        zero = _dependent_zero(finalize_slice(), norm_slice())
        zero = jnp.tile(zero, (v.shape[0] // SUBLANES, v.shape[1] // LANES))
        return pltpu.bitcast(pltpu.bitcast(v, jnp.uint32) | zero, F32)

    has_matmul = (r >= 1) & (r <= n_tiles)

    @pl.when(has_matmul)
    def _():
        k0 = 0
        for a in down_lhs(h_ref[mm_slot], anchor, r - 1):
            kc = a.shape[1]
            prev = jnp.where(j == 0, 0.0, acc_ref[mm_slot]) if k0 == 0 else acc_ref[mm_slot]
            acc_ref[mm_slot] = jnp.dot(a, wo_ref[k0:k0 + kc, :], preferred_element_type=F32) + prev
            k0 += kc

    @pl.when(jnp.logical_not(has_matmul))
    def _():
        finalize_slice()
        norm_slice()


def _cast_piece_rows(rows, steps):
    for pr in range(BF16_SUBLANES, rows + 1, BF16_SUBLANES):
        if rows % pr == 0 and rows // pr <= steps:
            return pr
    raise ValueError((rows, steps))


def _cast_jobs(cast_next, steps, step_index):
    arrays, in_specs, out_specs, out_shapes = [], [], [], []
    for w, lead in cast_next:
        rows, cols = w.shape[-2:]
        pr = _cast_piece_rows(rows, steps)

        def piece(*g, last=rows // pr - 1):
            return (jnp.minimum(step_index(*g), last), 0)

        arrays.append(w)
        in_specs.append(_stacked_spec(lead, (pr, cols), piece))
        out_specs.append(pl.BlockSpec((pr, cols), piece))
        out_shapes.append(jax.ShapeDtypeStruct((rows, cols), BF16))
    return arrays, in_specs, out_specs, out_shapes


def _with_cast_jobs(body, n_in, n_cast, *refs):
    for src, dst in zip(refs[n_in:n_in + n_cast], refs[n_in + n_cast + 1:n_in + 2 * n_cast + 1]):
        dst[...] = src[...].astype(BF16)
    body(*refs[:n_in], refs[n_in + n_cast], *refs[n_in + 2 * n_cast + 1:])


def _three_stage_call(name, body, x, g_pre, g_post, tm, nj, chunked_inputs, scratch_shapes,
                      cast_next):
    t, d = x.shape
    n_tiles = t // tm
    n_slices = _slices_per_tile(tm, nj)
    sr = tm // n_slices
    cast_arrays, cast_in, cast_out, cast_shapes = _cast_jobs(
        cast_next, (n_tiles + 2) * nj, lambda r, j: r * nj + j)

    def norm_slice(r, j):
        return (jnp.minimum(r, n_tiles - 1) * n_slices + jnp.minimum(j, n_slices - 1), 0)

    def final_slice(r, j):
        return (jnp.where(r < 2, 0, (r - 2) * n_slices + jnp.minimum(j, n_slices - 1)), 0)

    def chunk(r, j):
        return jnp.where((r >= 1) & (r <= n_tiles), j, 0)

    out, *cast = pl.pallas_call(
        functools.partial(_with_cast_jobs, functools.partial(body, n_tiles, n_slices),
                          4 + len(chunked_inputs), len(cast_arrays)),
        grid=(n_tiles + 2, nj),
        in_specs=[
            pl.BlockSpec((sr, d), norm_slice),
            pl.BlockSpec((sr, d), final_slice),
            pl.BlockSpec((1, d), lambda r, j: (0, 0)),
            pl.BlockSpec((1, d), lambda r, j: (0, 0)),
        ] + [spec_fn(chunk) for _, spec_fn in chunked_inputs] + cast_in,
        out_specs=[pl.BlockSpec((sr, d), final_slice)] + cast_out,
        out_shape=[jax.ShapeDtypeStruct((t, d), F32)] + cast_shapes,
        scratch_shapes=[pltpu.VMEM((2, tm, d), BF16), pltpu.VMEM((2, tm, d), F32)] + scratch_shapes,
        compiler_params=_params(("arbitrary", "arbitrary")),
        name=name,
    )(x, x, g_pre, g_post, *[a for a, _ in chunked_inputs], *cast_arrays)
    return out, cast


def _ffn_kernel(n_tiles, n_slices, xn_ref, xf_ref, gpre_ref, gpost_ref, wg_ref, wu_ref, wo_ref,
                o_ref, h_ref, acc_ref):
    def down_lhs(h, anchor, tile):
        groups = []
        for c in range(0, wg_ref.shape[1], MXU_WIDTH):
            g = jnp.dot(h, wg_ref[:, c:c + MXU_WIDTH], preferred_element_type=F32)
            u = jnp.dot(h, wu_ref[:, c:c + MXU_WIDTH], preferred_element_type=F32)
            if c == 0:
                u = anchor(u)
            groups.append((jax.nn.silu(g) * u).astype(BF16))
        return groups

    _three_stage_step(n_tiles, n_slices, 0.5, down_lhs, xn_ref, xf_ref, gpre_ref, gpost_ref,
                      wo_ref, o_ref, h_ref, acc_ref)


def _ffn(x, g_pre, g_post, w_in, w_out, cast_next):
    lead = ()
    t, d = x.shape
    f = w_out.shape[-2]
    tm = _pick_tile(t, 1024, BF16_SUBLANES)
    tf = _pick_tile(f, 512, LANES)
    nj = f // tf
    return _three_stage_call(
        "ffn_half_step", _ffn_kernel, x, g_pre, g_post, tm, nj,
        [(w_in, lambda chunk: _stacked_spec(lead, (d, tf), lambda r, j: (0, chunk(r, j)))),
         (w_in, lambda chunk: _stacked_spec(lead, (d, tf), lambda r, j: (0, nj + chunk(r, j)))),
         (w_out, lambda chunk: _stacked_spec(lead, (tf, d), lambda r, j: (chunk(r, j), 0)))],
        [], cast_next)


def _conv_kernel(tiles_per_seq, n_tiles, n_slices, xn_ref, xf_ref, gpre_ref, gpost_ref, wb_ref,
                 wc_ref, wu_ref, cw_ref, wo_ref, o_ref, h_ref, acc_ref, zbuf_ref, halo_ref):
    j = pl.program_id(1)
    tm = h_ref.shape[1]

    @pl.when((pl.program_id(0) == 0) & (j == 0))
    def _():
        halo_ref[...] = jnp.zeros_like(halo_ref)

    def down_lhs(h, anchor, tile):
        groups = []
        for c0 in range(0, wb_ref.shape[1], MXU_WIDTH):
            cols = slice(c0, c0 + MXU_WIDTH)
            b = jnp.dot(h, wb_ref[:, cols], preferred_element_type=F32)
            c = jnp.dot(h, wc_ref[:, cols], preferred_element_type=F32)
            u = jnp.dot(h, wu_ref[:, cols], preferred_element_type=F32)
            if c0 == 0:
                u = anchor(u)
            z = c * u

            zbuf_ref[pl.ds(SUBLANES, tm), cols] = z
            prev = halo_ref[j, :, cols]
            zbuf_ref[pl.ds(0, SUBLANES), cols] = jnp.where(tile % tiles_per_seq == 0,
                                                           jnp.zeros_like(prev), prev)
            halo_ref[j, :, cols] = z[tm - SUBLANES:, :]

            cw = cw_ref[:, cols]
            zc = (cw[0:1, :] * zbuf_ref[pl.ds(SUBLANES - 2, tm), cols]
                  + cw[1:2, :] * zbuf_ref[pl.ds(SUBLANES - 1, tm), cols]
                  + cw[2:3, :] * z)
            groups.append((b * zc).astype(BF16))
        return groups

    _three_stage_step(n_tiles, n_slices, 1.0, down_lhs, xn_ref, xf_ref, gpre_ref, gpost_ref,
                      wo_ref, o_ref, h_ref, acc_ref)


def _conv_mixer(x, seq, g_pre, g_post, w_in, conv_w, w_out, cast_next):
    lead = ()
    t, d = x.shape
    tm = _pick_tile(seq, 512, BF16_SUBLANES)
    tn = _pick_tile(d, 512, LANES)
    nj = d // tn

    def w_in_cols(part):
        return lambda chunk: _stacked_spec(lead, (d, tn), lambda r, j: (0, part * nj + chunk(r, j)))

    return _three_stage_call(
        "short_conv_mixer", functools.partial(_conv_kernel, seq // tm), x, g_pre, g_post, tm, nj,
        [(w_in, w_in_cols(0)), (w_in, w_in_cols(1)), (w_in, w_in_cols(2)),
         (conv_w, lambda chunk: _stacked_spec(lead, (CONV_WIDTH, tn), lambda r, j: (0, chunk(r, j)))),
         (w_out, lambda chunk: _stacked_spec(lead, (tn, d), lambda r, j: (chunk(r, j), 0)))],
        [pltpu.VMEM((tm + SUBLANES, tn), F32), pltpu.VMEM((nj, SUBLANES, tn), F32)], cast_next)


def _qkv_pass_col(j):
    return (2 * j) % 3


def _qkv_kernel(x_ref, gpre_ref, w_ref, cos_ref, sin_ref, o_ref, ksum_ref, h_ref):
    j = pl.program_id(1)
    tm, d = x_ref.shape
    half = ROT_DIM // 2
    nb = tm // MOBA_BLOCK

    @pl.when(j == 0)
    def _():
        h_ref[...] = _rms(x_ref[...], gpre_ref[...]).astype(BF16)

    rotary = j != 1
    cos = jnp.where(rotary, cos_ref[...], 1.0)
    sin = jnp.where(rotary, sin_ref[...], 0.0)
    lane = lax.broadcasted_iota(jnp.int32, (tm, HEAD_DIM), 1)
    pad_rows = jnp.zeros((SUBLANES - nb, HEAD_DIM), F32)
    h = h_ref[...]
    for c in range(d // MXU_WIDTH):
        r = jnp.dot(h, w_ref[:, c * MXU_WIDTH:(c + 1) * MXU_WIDTH], preferred_element_type=F32)
        for hh in range(MXU_WIDTH // HEAD_DIM):
            th = r[:, hh * HEAD_DIM:(hh + 1) * HEAD_DIM]
            partner = jnp.where(lane < half,
                                pltpu.roll(th, HEAD_DIM - half, 1),
                                pltpu.roll(th, half, 1))
            rot = th * cos + partner * sin
            cols = slice(c * MXU_WIDTH + hh * HEAD_DIM, c * MXU_WIDTH + (hh + 1) * HEAD_DIM)
            o_ref[:, cols] = rot.astype(BF16)
            sums = jnp.sum(rot.reshape(nb, MOBA_BLOCK, HEAD_DIM), axis=1)
            ksum_ref[0, :, cols] = jnp.concatenate([sums, pad_rows], axis=0) if nb < SUBLANES else sums


def _rope_tables(seq):
    half = ROT_DIM // 2
    inv_freq = ROPE_THETA ** (-jnp.arange(0, ROT_DIM, 2, dtype=F32) / ROT_DIM)
    ang = jnp.arange(seq, dtype=F32)[:, None] * inv_freq[None, :]
    cos, sin = jnp.cos(ang), jnp.sin(ang)
    pad = HEAD_DIM - ROT_DIM
    cos_t = jnp.concatenate([cos, cos, jnp.ones((seq, pad), F32)], axis=-1)
    sin_t = jnp.concatenate([-sin, sin, jnp.zeros((seq, pad), F32)], axis=-1)
    return cos_t, sin_t


def _qkv_proj(x, seq, g_pre, w_qkv):
    lead = ()
    t, d = x.shape
    tm = _pick_tile(seq, 1024, MOBA_BLOCK)
    assert tm // MOBA_BLOCK <= SUBLANES
    tiles_per_seq = seq // tm
    cos_t, sin_t = _rope_tables(seq)
    qkv, ksum = pl.pallas_call(
        _qkv_kernel,
        grid=(t // tm, 3),
        in_specs=[
            pl.BlockSpec((tm, d), lambda i, j: (i, 0)),
            pl.BlockSpec((1, d), lambda i, j: (0, 0)),
            _stacked_spec(lead, (d, d), lambda i, j: (0, _qkv_pass_col(j))),
            pl.BlockSpec((tm, HEAD_DIM), lambda i, j: (i % tiles_per_seq, 0)),
            pl.BlockSpec((tm, HEAD_DIM), lambda i, j: (i % tiles_per_seq, 0)),
        ],
        out_specs=[
            pl.BlockSpec((tm, d), lambda i, j: (i, _qkv_pass_col(j))),
            pl.BlockSpec((1, SUBLANES, d), lambda i, j: (i, 0, 0)),
        ],
        out_shape=[
            jax.ShapeDtypeStruct((t, 3 * d), BF16),
            jax.ShapeDtypeStruct((t // tm, SUBLANES, d), F32),
        ],
        scratch_shapes=[pltpu.VMEM((tm, d), BF16)],
        compiler_params=_params(("arbitrary", "arbitrary")),
        name="qkv_rope_proj",
    )(x, g_pre, w_qkv, cos_t, sin_t)
    ksum = ksum[:, :tm // MOBA_BLOCK, :].reshape(t // MOBA_BLOCK, d)
    return qkv, ksum


MOBA_HEADS_PER_STEP = 2


def _moba_kernel(q_ref, k_ref, v_ref, ksum_ref, o_ref, s_ref):
    for hh in range(q_ref.shape[1] // HEAD_DIM):
        cols = pl.ds(hh * HEAD_DIM, HEAD_DIM)
        _moba_head(q_ref.at[:, cols], k_ref.at[:, cols], v_ref.at[:, cols], ksum_ref.at[:, cols],
                   o_ref.at[:, cols], s_ref.at[hh])


def _moba_head(q_ref, k_ref, v_ref, ksum_ref, o_ref, s_ref):
    seq = q_ref.shape[0]
    nb = seq // MOBA_BLOCK
    blk = MOBA_BLOCK
    scale = HEAD_DIM ** -0.5
    nt = (((1,), (1,)), ((), ()))
    tn = (((0,), (0,)), ((), ()))

    kmean = (ksum_ref[...] * (1.0 / blk)).astype(BF16)
    gate = lax.dot_general(kmean, q_ref[...], nt, preferred_element_type=F32)
    n_idx = lax.broadcasted_iota(jnp.int32, (nb, seq), 0)
    q_blk = lax.broadcasted_iota(jnp.int32, (nb, seq), 1) // blk
    rank = jnp.zeros((nb, seq), jnp.int32)
    for m in range(nb - 1):
        gm = gate[m:m + 1, :]
        beats = (gm > gate) | ((gm == gate) & (m < n_idx))
        rank = rank + jnp.where(beats & (m < q_blk), 1, 0)
    sel = jnp.where((n_idx < q_blk) & (rank < MOBA_TOPK), 1.0, 0.0).astype(F32)

    key_i = lax.broadcasted_iota(jnp.int32, (blk, blk), 0)
    qry_i = lax.broadcasted_iota(jnp.int32, (blk, blk), 1)
    causal = key_i <= qry_i

    exp2_scale = scale * LOG2_E
    for i in reversed(range(nb)):
        slot = i % 2
        qi = q_ref[i * blk:(i + 1) * blk, :]
        m_run = None
        for jb in range(i + 1):
            kj = k_ref[jb * blk:(jb + 1) * blk, :]
            s = lax.dot_general(kj, qi, nt, preferred_element_type=F32)
            if jb == i:
                s = jnp.where(causal, s, NEG)
            else:
                s = jnp.where(sel[jb:jb + 1, i * blk:(i + 1) * blk] > 0.5, s, NEG)
            s_ref[slot, jb] = s
            m_run = s if m_run is None else jnp.maximum(m_run, s)
        m = jnp.max(m_run, axis=0, keepdims=True)
        l_run = None
        acc = None
        for jb in range(i + 1):
            p = jnp.exp2((s_ref[slot, jb] - m) * exp2_scale)
            l_run = p if l_run is None else l_run + p
            vj = v_ref[jb * blk:(jb + 1) * blk, :]
            pv = lax.dot_general(vj, p.astype(BF16), tn, preferred_element_type=F32)
            acc = pv if acc is None else acc + pv
        l = jnp.sum(l_run, axis=0, keepdims=True)
        o_ref[i * blk:(i + 1) * blk, :] = (acc / l).T.astype(BF16)


def _moba_attention(qkv, ksum, bsz, seq, cast_next):
    t, d3 = qkv.shape
    d = d3 // 3
    hp = MOBA_HEADS_PER_STEP if (d // HEAD_DIM) % MOBA_HEADS_PER_STEP == 0 else 1
    nh = d // (HEAD_DIM * hp)
    nb = seq // MOBA_BLOCK
    width = hp * HEAD_DIM
    cast_arrays, cast_in, cast_out, cast_shapes = _cast_jobs(
        cast_next, bsz * nh, lambda b, h: b * nh + h)
    out, *cast = pl.pallas_call(
        functools.partial(_with_cast_jobs, _moba_kernel, 4, len(cast_arrays)),
        grid=(bsz, nh),
        in_specs=[
            pl.BlockSpec((seq, width), lambda b, h: (b, h)),
            pl.BlockSpec((seq, width), lambda b, h: (b, nh + h)),
            pl.BlockSpec((seq, width), lambda b, h: (b, 2 * nh + h)),
            pl.BlockSpec((nb, width), lambda b, h: (b, h)),
        ] + cast_in,
        out_specs=[pl.BlockSpec((seq, width), lambda b, h: (b, h))] + cast_out,
        out_shape=[jax.ShapeDtypeStruct((t, d), BF16)] + cast_shapes,
        scratch_shapes=[pltpu.VMEM((hp, 2, nb, MOBA_BLOCK, MOBA_BLOCK), F32)],
        compiler_params=_params(("arbitrary", "arbitrary")),
        name="moba_attention",
    )(qkv, qkv, qkv, ksum, *cast_arrays)
    return out, cast


def _oproj_kernel(x_ref, o_ref, w_ref, gpost_ref, out_ref):
    m = jnp.dot(o_ref[...], w_ref[...], preferred_element_type=F32)
    out_ref[...] = x_ref[...] + _rms(m, gpost_ref[...])


def _out_proj(x, o, w_o, g_post):
    lead = ()
    t, d = x.shape
    tm = _pick_tile(t, 512, SUBLANES)
    return pl.pallas_call(
        _oproj_kernel,
        grid=(t // tm,),
        in_specs=[
            pl.BlockSpec((tm, d), lambda i: (i, 0)),
            pl.BlockSpec((tm, d), lambda i: (i, 0)),
            _stacked_spec(lead, (d, d), lambda i: (0, 0)),
            pl.BlockSpec((1, d), lambda i: (0, 0)),
        ],
        out_specs=pl.BlockSpec((tm, d), lambda i: (i, 0)),
        out_shape=jax.ShapeDtypeStruct((t, d), F32),
        compiler_params=_params(("arbitrary",)),
        name="attn_out_proj",
    )(x, o, w_o, g_post)


def kernel(x, norm_g, ffn_w_in, ffn_w_out, conv_w_in, conv_w, conv_w_out, attn_w_qkv, attn_w_o):
    bsz, seq, d = x.shape
    depth = norm_g.shape[0]
    assert d % HEAD_DIM == 0 and seq % MOBA_BLOCK == 0
    assert min(MOBA_TOPK, seq // MOBA_BLOCK - 1) == MOBA_TOPK
    h = x.reshape(bsz * seq, d)

    def ffn_weights(i, half):
        return [(ffn_w_in, (i, half)), (ffn_w_out, (i, half))]

    def mixer_weights(i):
        if i % 2 == 0:
            return [(conv_w_in, (i // 2,)), (conv_w_out, (i // 2,))]
        return [(attn_w_qkv, (i // 2,)), (attn_w_o, (i // 2,))]

    w_in, w_out = ffn_w_in[0, 0].astype(BF16), ffn_w_out[0, 0].astype(BF16)
    for i in range(depth):
        g = norm_g[i][:, None, :]
        h, (wm_a, wm_b) = _ffn(h, g[0], g[1], w_in, w_out, mixer_weights(i))
        if i % 2 == 0:
            h, (w_in, w_out) = _conv_mixer(h, seq, g[2], g[3], wm_a, conv_w[i // 2], wm_b,
                                           ffn_weights(i, 1))
        else:
            qkv, ksum = _qkv_proj(h, seq, g[2], wm_a)
            o, (w_in, w_out) = _moba_attention(qkv, ksum, bsz, seq, ffn_weights(i, 1))
            h = _out_proj(h, o, wm_b, g[3])
        last = i == depth - 1
        h, nxt = _ffn(h, g[4], g[5], w_in, w_out, [] if last else ffn_weights(i + 1, 0))
        if not last:
            w_in, w_out = nxt
    return h.reshape(bsz, seq, d)
```

```python
import functools

import jax
import jax.numpy as jnp
from jax import lax
from jax.experimental import pallas as pl
from jax.experimental.pallas import tpu as pltpu

HEAD_DIM = 128
ROT_DIM = HEAD_DIM // 4
ROPE_THETA = 500000.0
MOBA_BLOCK = 256
MOBA_TOPK = 3
CONV_WIDTH = 3
RMS_EPS = 1e-6
NEG = -1e30
LOG2_E = 1.4426950408889634

LANES = 128
SUBLANES = 8
MXU_WIDTH = 256
V7X_VMEM_LIMIT_BYTES = 56 * 1024 * 1024

F32 = jnp.float32
BF16 = jnp.bfloat16


def _pick_tile(n, target, multiple):
    t = min(n, target)
    t -= t % multiple
    while t > multiple and n % t:
        t -= multiple
    assert t > 0 and n % t == 0, (n, target, multiple)
    return t


def _rms(x, g):
    ms = jnp.mean(x * x, axis=-1, keepdims=True)
    return (x * lax.rsqrt(ms + RMS_EPS)) * g


def _params(sem):
    return pltpu.CompilerParams(dimension_semantics=sem,
                                vmem_limit_bytes=V7X_VMEM_LIMIT_BYTES)


def _stacked_spec(lead, block, index_fn):
    lead = tuple(lead)
    return pl.BlockSpec((None,) * len(lead) + tuple(block), lambda *g: lead + tuple(index_fn(*g)))


BF16_SUBLANES = 16


def _slices_per_tile(tm, nj):
    n = min(nj, tm // BF16_SUBLANES)
    while tm % n or (tm // n) % BF16_SUBLANES:
        n -= 1
    return n


def _dependent_zero(*vals):
    words = []
    for v in vals:
        bits = pltpu.bitcast(v, jnp.uint32)
        words += [bits[s:s + SUBLANES, c:c + LANES]
                  for c in range(0, bits.shape[1], LANES) for s in range(0, bits.shape[0], SUBLANES)]
    while len(words) > 1:
        words = [a | b for a, b in zip(words[0::2], words[1::2])] + words[len(words) & ~1:]
    return (words[0] >> 16) >> 16


def _three_stage_step(n_tiles, n_slices, res_scale, down_lhs, xn_ref, xf_ref, gpre_ref, gpost_ref,
                      wo_ref, o_ref, h_ref, acc_ref):
    r = pl.program_id(0)
    j = pl.program_id(1)
    sr = xn_ref.shape[0]
    rows = pl.ds(pl.multiple_of(jnp.minimum(j, n_slices - 1) * sr, sr), sr)
    norm_slot = r % 2
    mm_slot = 1 - norm_slot

    @pl.when((r == 0) & (j == 0))
    def _():
        acc_ref[...] = jnp.zeros_like(acc_ref)

    def finalize_slice():
        out = xf_ref[...] + res_scale * _rms(acc_ref[norm_slot, rows, :], gpost_ref[...])
        o_ref[...] = out
        return out

    def norm_slice():
        hn = _rms(xn_ref[...], gpre_ref[...])
        h_ref[norm_slot, rows, :] = hn.astype(BF16)
        return hn

    def anchor(v):
        zero = _dependent_zero(finalize_slice(), norm_slice())
        zero = jnp.tile(zero, (v.shape[0] // SUBLANES, v.shape[1] // LANES))
        return pltpu.bitcast(pltpu.bitcast(v, jnp.uint32) | zero, F32)

    has_matmul = (r >= 1) & (r <= n_tiles)

    @pl.when(has_matmul)
    def _():
        k0 = 0
        for a in down_lhs(h_ref[mm_slot], anchor, r - 1):
            kc = a.shape[1]
            prev = jnp.where(j == 0, 0.0, acc_ref[mm_slot]) if k0 == 0 else acc_ref[mm_slot]
            acc_ref[mm_slot] = jnp.dot(a, wo_ref[k0:k0 + kc, :], preferred_element_type=F32) + prev
            k0 += kc

    @pl.when(jnp.logical_not(has_matmul))
    def _():
        finalize_slice()
        norm_slice()


def _cast_piece_rows(rows, steps):
    for pr in range(BF16_SUBLANES, rows + 1, BF16_SUBLANES):
        if rows % pr == 0 and rows // pr <= steps:
            return pr
    raise ValueError((rows, steps))


def _cast_jobs(cast_next, steps, step_index):
    arrays, in_specs, out_specs, out_shapes = [], [], [], []
    for w, lead in cast_next:
        rows, cols = w.shape[-2:]
        pr = _cast_piece_rows(rows, steps)

        def piece(*g, last=rows // pr - 1):
            return (jnp.minimum(step_index(*g), last), 0)

        arrays.append(w)
        in_specs.append(_stacked_spec(lead, (pr, cols), piece))
        out_specs.append(pl.BlockSpec((pr, cols), piece))
        out_shapes.append(jax.ShapeDtypeStruct((rows, cols), BF16))
    return arrays, in_specs, out_specs, out_shapes


def _with_cast_jobs(body, n_in, n_cast, *refs):
    for src, dst in zip(refs[n_in:n_in + n_cast], refs[n_in + n_cast + 1:n_in + 2 * n_cast + 1]):
        dst[...] = src[...].astype(BF16)
    body(*refs[:n_in], refs[n_in + n_cast], *refs[n_in + 2 * n_cast + 1:])


def _three_stage_call(name, body, x, g_pre, g_post, tm, nj, chunked_inputs, scratch_shapes,
                      cast_next):
    t, d = x.shape
    n_tiles = t // tm
    n_slices = _slices_per_tile(tm, nj)
    sr = tm // n_slices
    cast_arrays, cast_in, cast_out, cast_shapes = _cast_jobs(
        cast_next, (n_tiles + 2) * nj, lambda r, j: r * nj + j)

    def norm_slice(r, j):
        return (jnp.minimum(r, n_tiles - 1) * n_slices + jnp.minimum(j, n_slices - 1), 0)

    def final_slice(r, j):
        return (jnp.where(r < 2, 0, (r - 2) * n_slices + jnp.minimum(j, n_slices - 1)), 0)

    def chunk(r, j):
        return jnp.where((r >= 1) & (r <= n_tiles), j, 0)

    out, *cast = pl.pallas_call(
        functools.partial(_with_cast_jobs, functools.partial(body, n_tiles, n_slices),
                          4 + len(chunked_inputs), len(cast_arrays)),
        grid=(n_tiles + 2, nj),
        in_specs=[
            pl.BlockSpec((sr, d), norm_slice),
            pl.BlockSpec((sr, d), final_slice),
            pl.BlockSpec((1, d), lambda r, j: (0, 0)),
            pl.BlockSpec((1, d), lambda r, j: (0, 0)),
        ] + [spec_fn(chunk) for _, spec_fn in chunked_inputs] + cast_in,
        out_specs=[pl.BlockSpec((sr, d), final_slice)] + cast_out,
        out_shape=[jax.ShapeDtypeStruct((t, d), F32)] + cast_shapes,
        scratch_shapes=[pltpu.VMEM((2, tm, d), BF16), pltpu.VMEM((2, tm, d), F32)] + scratch_shapes,
        compiler_params=_params(("arbitrary", "arbitrary")),
        name=name,
    )(x, x, g_pre, g_post, *[a for a, _ in chunked_inputs], *cast_arrays)
    return out, cast


def _ffn_kernel(n_tiles, n_slices, xn_ref, xf_ref, gpre_ref, gpost_ref, wg_ref, wu_ref, wo_ref,
                o_ref, h_ref, acc_ref):
    def down_lhs(h, anchor, tile):
        groups = []
        for c in range(0, wg_ref.shape[1], MXU_WIDTH):
            g = jnp.dot(h, wg_ref[:, c:c + MXU_WIDTH], preferred_element_type=F32)
            u = jnp.dot(h, wu_ref[:, c:c + MXU_WIDTH], preferred_element_type=F32)
            if c == 0:
                u = anchor(u)
            groups.append((jax.nn.silu(g) * u).astype(BF16))
        return groups

    _three_stage_step(n_tiles, n_slices, 0.5, down_lhs, xn_ref, xf_ref, gpre_ref, gpost_ref,
                      wo_ref, o_ref, h_ref, acc_ref)


def _ffn(x, g_pre, g_post, w_in, w_out, cast_next):
    lead = ()
    t, d = x.shape
    f = w_out.shape[-2]
    tm = _pick_tile(t, 1024, BF16_SUBLANES)
    tf = _pick_tile(f, 512, LANES)
    nj = f // tf
    return _three_stage_call(
        "ffn_half_step", _ffn_kernel, x, g_pre, g_post, tm, nj,
        [(w_in, lambda chunk: _stacked_spec(lead, (d, tf), lambda r, j: (0, chunk(r, j)))),
         (w_in, lambda chunk: _stacked_spec(lead, (d, tf), lambda r, j: (0, nj + chunk(r, j)))),
         (w_out, lambda chunk: _stacked_spec(lead, (tf, d), lambda r, j: (chunk(r, j), 0)))],
        [], cast_next)


def _conv_kernel(tiles_per_seq, n_tiles, n_slices, xn_ref, xf_ref, gpre_ref, gpost_ref, wb_ref,
                 wc_ref, wu_ref, cw_ref, wo_ref, o_ref, h_ref, acc_ref, zbuf_ref, halo_ref):
    j = pl.program_id(1)
    tm = h_ref.shape[1]

    @pl.when((pl.program_id(0) == 0) & (j == 0))
    def _():
        halo_ref[...] = jnp.zeros_like(halo_ref)

    def down_lhs(h, anchor, tile):
        groups = []
        for c0 in range(0, wb_ref.shape[1], MXU_WIDTH):
            cols = slice(c0, c0 + MXU_WIDTH)
            b = jnp.dot(h, wb_ref[:, cols], preferred_element_type=F32)
            c = jnp.dot(h, wc_ref[:, cols], preferred_element_type=F32)
            u = jnp.dot(h, wu_ref[:, cols], preferred_element_type=F32)
            if c0 == 0:
                u = anchor(u)
            z = c * u

            zbuf_ref[pl.ds(SUBLANES, tm), cols] = z
            prev = halo_ref[j, :, cols]
            zbuf_ref[pl.ds(0, SUBLANES), cols] = jnp.where(tile % tiles_per_seq == 0,
                                                           jnp.zeros_like(prev), prev)
            halo_ref[j, :, cols] = z[tm - SUBLANES:, :]

            cw = cw_ref[:, cols]
            zc = (cw[0:1, :] * zbuf_ref[pl.ds(SUBLANES - 2, tm), cols]
                  + cw[1:2, :] * zbuf_ref[pl.ds(SUBLANES - 1, tm), cols]
                  + cw[2:3, :] * z)
            groups.append((b * zc).astype(BF16))
        return groups

    _three_stage_step(n_tiles, n_slices, 1.0, down_lhs, xn_ref, xf_ref, gpre_ref, gpost_ref,
                      wo_ref, o_ref, h_ref, acc_ref)


def _conv_mixer(x, seq, g_pre, g_post, w_in, conv_w, w_out, cast_next):
    lead = ()
    t, d = x.shape
    tm = _pick_tile(seq, 512, BF16_SUBLANES)
    tn = _pick_tile(d, 512, LANES)
    nj = d // tn

    def w_in_cols(part):
        return lambda chunk: _stacked_spec(lead, (d, tn), lambda r, j: (0, part * nj + chunk(r, j)))

    return _three_stage_call(
        "short_conv_mixer", functools.partial(_conv_kernel, seq // tm), x, g_pre, g_post, tm, nj,
        [(w_in, w_in_cols(0)), (w_in, w_in_cols(1)), (w_in, w_in_cols(2)),
         (conv_w, lambda chunk: _stacked_spec(lead, (CONV_WIDTH, tn), lambda r, j: (0, chunk(r, j)))),
         (w_out, lambda chunk: _stacked_spec(lead, (tn, d), lambda r, j: (chunk(r, j), 0)))],
        [pltpu.VMEM((tm + SUBLANES, tn), F32), pltpu.VMEM((nj, SUBLANES, tn), F32)], cast_next)


def _qkv_pass_col(j):
    return (2 * j) % 3


def _qkv_kernel(x_ref, gpre_ref, w_ref, cos_ref, sin_ref, o_ref, ksum_ref, h_ref):
    j = pl.program_id(1)
    tm, d = x_ref.shape
    half = ROT_DIM // 2
    nb = tm // MOBA_BLOCK

    @pl.when(j == 0)
    def _():
        h_ref[...] = _rms(x_ref[...], gpre_ref[...]).astype(BF16)

    rotary = j != 1
    cos = jnp.where(rotary, cos_ref[...], 1.0)
    sin = jnp.where(rotary, sin_ref[...], 0.0)
    lane = lax.broadcasted_iota(jnp.int32, (tm, HEAD_DIM), 1)
    pad_rows = jnp.zeros((SUBLANES - nb, HEAD_DIM), F32)
    h = h_ref[...]
    for c in range(d // MXU_WIDTH):
        r = jnp.dot(h, w_ref[:, c * MXU_WIDTH:(c + 1) * MXU_WIDTH], preferred_element_type=F32)
        for hh in range(MXU_WIDTH // HEAD_DIM):
            th = r[:, hh * HEAD_DIM:(hh + 1) * HEAD_DIM]
            partner = jnp.where(lane < half,
                                pltpu.roll(th, HEAD_DIM - half, 1),
                                pltpu.roll(th, half, 1))
            rot = th * cos + partner * sin
            cols = slice(c * MXU_WIDTH + hh * HEAD_DIM, c * MXU_WIDTH + (hh + 1) * HEAD_DIM)
            o_ref[:, cols] = rot.astype(BF16)
            sums = jnp.sum(rot.reshape(nb, MOBA_BLOCK, HEAD_DIM), axis=1)
            ksum_ref[0, :, cols] = jnp.concatenate([sums, pad_rows], axis=0) if nb < SUBLANES else sums


def _rope_tables(seq):
    half = ROT_DIM // 2
    inv_freq = ROPE_THETA ** (-jnp.arange(0, ROT_DIM, 2, dtype=F32) / ROT_DIM)
    ang = jnp.arange(seq, dtype=F32)[:, None] * inv_freq[None, :]
    cos, sin = jnp.cos(ang), jnp.sin(ang)
    pad = HEAD_DIM - ROT_DIM
    cos_t = jnp.concatenate([cos, cos, jnp.ones((seq, pad), F32)], axis=-1)
    sin_t = jnp.concatenate([-sin, sin, jnp.zeros((seq, pad), F32)], axis=-1)
    return cos_t, sin_t


def _qkv_proj(x, seq, g_pre, w_qkv):
    lead = ()
    t, d = x.shape
    tm = _pick_tile(seq, 1024, MOBA_BLOCK)
    assert tm // MOBA_BLOCK <= SUBLANES
    tiles_per_seq = seq // tm
    cos_t, sin_t = _rope_tables(seq)
    qkv, ksum = pl.pallas_call(
        _qkv_kernel,
        grid=(t // tm, 3),
        in_specs=[
            pl.BlockSpec((tm, d), lambda i, j: (i, 0)),
            pl.BlockSpec((1, d), lambda i, j: (0, 0)),
            _stacked_spec(lead, (d, d), lambda i, j: (0, _qkv_pass_col(j))),
            pl.BlockSpec((tm, HEAD_DIM), lambda i, j: (i % tiles_per_seq, 0)),
            pl.BlockSpec((tm, HEAD_DIM), lambda i, j: (i % tiles_per_seq, 0)),
        ],
        out_specs=[
            pl.BlockSpec((tm, d), lambda i, j: (i, _qkv_pass_col(j))),
            pl.BlockSpec((1, SUBLANES, d), lambda i, j: (i, 0, 0)),
        ],
        out_shape=[
            jax.ShapeDtypeStruct((t, 3 * d), BF16),
            jax.ShapeDtypeStruct((t // tm, SUBLANES, d), F32),
        ],
        scratch_shapes=[pltpu.VMEM((tm, d), BF16)],
        compiler_params=_params(("arbitrary", "arbitrary")),
        name="qkv_rope_proj",
    )(x, g_pre, w_qkv, cos_t, sin_t)
    ksum = ksum[:, :tm // MOBA_BLOCK, :].reshape(t // MOBA_BLOCK, d)
    return qkv, ksum


MOBA_HEADS_PER_STEP = 2


def _moba_kernel(q_ref, k_ref, v_ref, ksum_ref, o_ref, s_ref):
    heads = []
    for hh in range(q_ref.shape[1] // HEAD_DIM):
        cols = pl.ds(hh * HEAD_DIM, HEAD_DIM)
        heads.append((q_ref.at[:, cols], k_ref.at[:, cols], v_ref.at[:, cols], ksum_ref.at[:, cols],
                      o_ref.at[:, cols], s_ref.at[hh]))
    maxes = [_moba_scores(q, k, ksum, s) for q, k, v, ksum, o, s in heads]
    for (q, k, v, ksum, o, s), m_of in zip(heads, maxes):
        _moba_outputs(v, o, s, m_of)


def _tile_index(i, jb):
    return i * (i + 1) // 2 + jb


def _moba_scores(q_ref, k_ref, ksum_ref, s_ref):
    seq = q_ref.shape[0]
    nb = seq // MOBA_BLOCK
    blk = MOBA_BLOCK
    nt = (((1,), (1,)), ((), ()))

    kmean = (ksum_ref[...] * (1.0 / blk)).astype(BF16)
    gate = lax.dot_general(kmean, q_ref[...], nt, preferred_element_type=F32)
    n_idx = lax.broadcasted_iota(jnp.int32, (nb, seq), 0)
    q_blk = lax.broadcasted_iota(jnp.int32, (nb, seq), 1) // blk
    rank = jnp.zeros((nb, seq), jnp.int32)
    for m in range(nb - 1):
        gm = gate[m:m + 1, :]
        beats = (gm > gate) | ((gm == gate) & (m < n_idx))
        rank = rank + jnp.where(beats & (m < q_blk), 1, 0)
    sel = jnp.where((n_idx < q_blk) & (rank < MOBA_TOPK), 1.0, 0.0).astype(F32)

    key_i = lax.broadcasted_iota(jnp.int32, (blk, blk), 0)
    qry_i = lax.broadcasted_iota(jnp.int32, (blk, blk), 1)
    causal = key_i <= qry_i

    m_of = {}
    for i in reversed(range(nb)):
        qi = q_ref[i * blk:(i + 1) * blk, :]
        m_run = None
        for jb in range(i + 1):
            kj = k_ref[jb * blk:(jb + 1) * blk, :]
            s = lax.dot_general(kj, qi, nt, preferred_element_type=F32)
            if jb == i:
                s = jnp.where(causal, s, NEG)
            else:
                s = jnp.where(sel[jb:jb + 1, i * blk:(i + 1) * blk] > 0.5, s, NEG)
            s_ref[_tile_index(i, jb)] = s
            m_run = s if m_run is None else jnp.maximum(m_run, s)
        m_of[i] = jnp.max(m_run, axis=0, keepdims=True)
    return m_of


def _moba_outputs(v_ref, o_ref, s_ref, m_of):
    blk = MOBA_BLOCK
    tn = (((0,), (0,)), ((), ()))
    exp2_scale = HEAD_DIM ** -0.5 * LOG2_E
    for i in sorted(m_of, reverse=True):
        l_run = None
        acc = None
        for jb in range(i + 1):
            p = jnp.exp2((s_ref[_tile_index(i, jb)] - m_of[i]) * exp2_scale)
            l_run = p if l_run is None else l_run + p
            vj = v_ref[jb * blk:(jb + 1) * blk, :]
            pv = lax.dot_general(vj, p.astype(BF16), tn, preferred_element_type=F32)
            acc = pv if acc is None else acc + pv
        l = jnp.sum(l_run, axis=0, keepdims=True)
        o_ref[i * blk:(i + 1) * blk, :] = (acc / l).T.astype(BF16)


def _moba_attention(qkv, ksum, bsz, seq, cast_next):
    t, d3 = qkv.shape
    d = d3 // 3
    hp = MOBA_HEADS_PER_STEP if (d // HEAD_DIM) % MOBA_HEADS_PER_STEP == 0 else 1
    nh = d // (HEAD_DIM * hp)
    nb = seq // MOBA_BLOCK
    width = hp * HEAD_DIM
    cast_arrays, cast_in, cast_out, cast_shapes = _cast_jobs(
        cast_next, bsz * nh, lambda b, h: b * nh + h)
    out, *cast = pl.pallas_call(
        functools.partial(_with_cast_jobs, _moba_kernel, 4, len(cast_arrays)),
        grid=(bsz, nh),
        in_specs=[
            pl.BlockSpec((seq, width), lambda b, h: (b, h)),
            pl.BlockSpec((seq, width), lambda b, h: (b, nh + h)),
            pl.BlockSpec((seq, width), lambda b, h: (b, 2 * nh + h)),
            pl.BlockSpec((nb, width), lambda b, h: (b, h)),
        ] + cast_in,
        out_specs=[pl.BlockSpec((seq, width), lambda b, h: (b, h))] + cast_out,
        out_shape=[jax.ShapeDtypeStruct((t, d), BF16)] + cast_shapes,
        scratch_shapes=[pltpu.VMEM((hp, nb * (nb + 1) // 2, MOBA_BLOCK, MOBA_BLOCK), F32)],
        compiler_params=_params(("arbitrary", "arbitrary")),
        name="moba_attention",
    )(qkv, qkv, qkv, ksum, *cast_arrays)
    return out, cast


def _oproj_kernel(x_ref, o_ref, w_ref, gpost_ref, out_ref):
    m = jnp.dot(o_ref[...], w_ref[...], preferred_element_type=F32)
    out_ref[...] = x_ref[...] + _rms(m, gpost_ref[...])


def _out_proj(x, o, w_o, g_post):
    lead = ()
    t, d = x.shape
    tm = _pick_tile(t, 512, SUBLANES)
    return pl.pallas_call(
        _oproj_kernel,
        grid=(t // tm,),
        in_specs=[
            pl.BlockSpec((tm, d), lambda i: (i, 0)),
            pl.BlockSpec((tm, d), lambda i: (i, 0)),
            _stacked_spec(lead, (d, d), lambda i: (0, 0)),
            pl.BlockSpec((1, d), lambda i: (0, 0)),
        ],
        out_specs=pl.BlockSpec((tm, d), lambda i: (i, 0)),
        out_shape=jax.ShapeDtypeStruct((t, d), F32),
        compiler_params=_params(("arbitrary",)),
        name="attn_out_proj",
    )(x, o, w_o, g_post)


def kernel(x, norm_g, ffn_w_in, ffn_w_out, conv_w_in, conv_w, conv_w_out, attn_w_qkv, attn_w_o):
    bsz, seq, d = x.shape
    depth = norm_g.shape[0]
    assert d % HEAD_DIM == 0 and seq % MOBA_BLOCK == 0
    assert min(MOBA_TOPK, seq // MOBA_BLOCK - 1) == MOBA_TOPK
    h = x.reshape(bsz * seq, d)

    def ffn_weights(i, half):
        return [(ffn_w_in, (i, half)), (ffn_w_out, (i, half))]

    def mixer_weights(i):
        if i % 2 == 0:
            return [(conv_w_in, (i // 2,)), (conv_w_out, (i // 2,))]
        return [(attn_w_qkv, (i // 2,)), (attn_w_o, (i // 2,))]

    w_in, w_out = ffn_w_in[0, 0].astype(BF16), ffn_w_out[0, 0].astype(BF16)
    for i in range(depth):
        g = norm_g[i][:, None, :]
        h, (wm_a, wm_b) = _ffn(h, g[0], g[1], w_in, w_out, mixer_weights(i))
        if i % 2 == 0:
            h, (w_in, w_out) = _conv_mixer(h, seq, g[2], g[3], wm_a, conv_w[i // 2], wm_b,
                                           ffn_weights(i, 1))
        else:
            qkv, ksum = _qkv_proj(h, seq, g[2], wm_a)
            o, (w_in, w_out) = _moba_attention(qkv, ksum, bsz, seq, ffn_weights(i, 1))
            h = _out_proj(h, o, wm_b, g[3])
        last = i == depth - 1
        h, nxt = _ffn(h, g[4], g[5], w_in, w_out, [] if last else ffn_weights(i + 1, 0))
        if not last:
            w_in, w_out = nxt
    return h.reshape(bsz, seq, d)
```

```python
import functools

import jax
import jax.numpy as jnp
from jax import lax
from jax.experimental import pallas as pl
from jax.experimental.pallas import tpu as pltpu

HEAD_DIM = 128
ROT_DIM = HEAD_DIM // 4
ROPE_THETA = 500000.0
MOBA_BLOCK = 256
MOBA_TOPK = 3
CONV_WIDTH = 3
RMS_EPS = 1e-6
NEG = -1e30
LOG2_E = 1.4426950408889634

LANES = 128
SUBLANES = 8
MXU_WIDTH = 256
V7X_VMEM_LIMIT_BYTES = 56 * 1024 * 1024

F32 = jnp.float32
BF16 = jnp.bfloat16


def _pick_tile(n, target, multiple):
    t = min(n, target)
    t -= t % multiple
    while t > multiple and n % t:
        t -= multiple
    assert t > 0 and n % t == 0, (n, target, multiple)
    return t


def _rms(x, g):
    ms = jnp.mean(x * x, axis=-1, keepdims=True)
    return (x * lax.rsqrt(ms + RMS_EPS)) * g


def _params(sem):
    return pltpu.CompilerParams(dimension_semantics=sem,
                                vmem_limit_bytes=V7X_VMEM_LIMIT_BYTES)


def _stacked_spec(lead, block, index_fn):
    lead = tuple(lead)
    return pl.BlockSpec((None,) * len(lead) + tuple(block), lambda *g: lead + tuple(index_fn(*g)))


BF16_SUBLANES = 16


def _slices_per_tile(tm, nj):
    n = min(nj, tm // BF16_SUBLANES)
    while tm % n or (tm // n) % BF16_SUBLANES:
        n -= 1
    return n


def _dependent_zero(*vals):
    words = []
    for v in vals:
        bits = pltpu.bitcast(v, jnp.uint32)
        words += [bits[s:s + SUBLANES, c:c + LANES]
                  for c in range(0, bits.shape[1], LANES) for s in range(0, bits.shape[0], SUBLANES)]
    while len(words) > 1:
        words = [a | b for a, b in zip(words[0::2], words[1::2])] + words[len(words) & ~1:]
    return (words[0] >> 16) >> 16


def _three_stage_step(n_tiles, n_slices, res_scale, down_lhs, xn_ref, xf_ref, gpre_ref, gpost_ref,
                      wo_ref, o_ref, h_ref, acc_ref):
    r = pl.program_id(0)
    j = pl.program_id(1)
    sr = xn_ref.shape[0]
    rows = pl.ds(pl.multiple_of(jnp.minimum(j, n_slices - 1) * sr, sr), sr)
    norm_slot = r % 2
    mm_slot = 1 - norm_slot

    @pl.when((r == 0) & (j == 0))
    def _():
        acc_ref[...] = jnp.zeros_like(acc_ref)

    def finalize_slice():
        out = xf_ref[...] + res_scale * _rms(acc_ref[norm_slot, rows, :], gpost_ref[...])
        o_ref[...] = out
        return out

    def norm_slice():
        hn = _rms(xn_ref[...], gpre_ref[...])
        h_ref[norm_slot, rows, :] = hn.astype(BF16)
        return hn

    def anchor(v):
        zero = _dependent_zero(finalize_slice(), norm_slice())
        zero = jnp.tile(zero, (v.shape[0] // SUBLANES, v.shape[1] // LANES))
        return pltpu.bitcast(pltpu.bitcast(v, jnp.uint32) | zero, F32)

    has_matmul = (r >= 1) & (r <= n_tiles)

    @pl.when(has_matmul)
    def _():
        k0 = 0
        for a in down_lhs(h_ref[mm_slot], anchor, r - 1):
            kc = a.shape[1]
            prev = jnp.where(j == 0, 0.0, acc_ref[mm_slot]) if k0 == 0 else acc_ref[mm_slot]
            acc_ref[mm_slot] = jnp.dot(a, wo_ref[k0:k0 + kc, :], preferred_element_type=F32) + prev
            k0 += kc

    @pl.when(jnp.logical_not(has_matmul))
    def _():
        finalize_slice()
        norm_slice()


def _cast_piece_rows(rows, steps):
    for pr in range(BF16_SUBLANES, rows + 1, BF16_SUBLANES):
        if rows % pr == 0 and rows // pr <= steps:
            return pr
    raise ValueError((rows, steps))


def _cast_jobs(cast_next, steps, step_index):
    arrays, in_specs, out_specs, out_shapes = [], [], [], []
    for w, lead in cast_next:
        rows, cols = w.shape[-2:]
        pr = _cast_piece_rows(rows, steps)

        def piece(*g, last=rows // pr - 1):
            return (jnp.minimum(step_index(*g), last), 0)

        arrays.append(w)
        in_specs.append(_stacked_spec(lead, (pr, cols), piece))
        out_specs.append(pl.BlockSpec((pr, cols), piece))
        out_shapes.append(jax.ShapeDtypeStruct((rows, cols), BF16))
    return arrays, in_specs, out_specs, out_shapes


def _with_cast_jobs(body, n_in, n_cast, *refs):
    for src, dst in zip(refs[n_in:n_in + n_cast], refs[n_in + n_cast + 1:n_in + 2 * n_cast + 1]):
        dst[...] = src[...].astype(BF16)
    body(*refs[:n_in], refs[n_in + n_cast], *refs[n_in + 2 * n_cast + 1:])


def _three_stage_call(name, body, x, g_pre, g_post, tm, nj, chunked_inputs, scratch_shapes,
                      cast_next):
    t, d = x.shape
    n_tiles = t // tm
    n_slices = _slices_per_tile(tm, nj)
    sr = tm // n_slices
    cast_arrays, cast_in, cast_out, cast_shapes = _cast_jobs(
        cast_next, (n_tiles + 2) * nj, lambda r, j: r * nj + j)

    def norm_slice(r, j):
        return (jnp.minimum(r, n_tiles - 1) * n_slices + jnp.minimum(j, n_slices - 1), 0)

    def final_slice(r, j):
        return (jnp.where(r < 2, 0, (r - 2) * n_slices + jnp.minimum(j, n_slices - 1)), 0)

    def chunk(r, j):
        return jnp.where((r >= 1) & (r <= n_tiles), j, 0)

    out, *cast = pl.pallas_call(
        functools.partial(_with_cast_jobs, functools.partial(body, n_tiles, n_slices),
                          4 + len(chunked_inputs), len(cast_arrays)),
        grid=(n_tiles + 2, nj),
        in_specs=[
            pl.BlockSpec((sr, d), norm_slice),
            pl.BlockSpec((sr, d), final_slice),
            pl.BlockSpec((1, d), lambda r, j: (0, 0)),
            pl.BlockSpec((1, d), lambda r, j: (0, 0)),
        ] + [spec_fn(chunk) for _, spec_fn in chunked_inputs] + cast_in,
        out_specs=[pl.BlockSpec((sr, d), final_slice)] + cast_out,
        out_shape=[jax.ShapeDtypeStruct((t, d), F32)] + cast_shapes,
        scratch_shapes=[pltpu.VMEM((2, tm, d), BF16), pltpu.VMEM((2, tm, d), F32)] + scratch_shapes,
        compiler_params=_params(("arbitrary", "arbitrary")),
        name=name,
    )(x, x, g_pre, g_post, *[a for a, _ in chunked_inputs], *cast_arrays)
    return out, cast


def _ffn_kernel(n_tiles, n_slices, xn_ref, xf_ref, gpre_ref, gpost_ref, wg_ref, wu_ref, wo_ref,
                o_ref, h_ref, acc_ref):
    def down_lhs(h, anchor, tile):
        groups = []
        for c in range(0, wg_ref.shape[1], MXU_WIDTH):
            g = jnp.dot(h, wg_ref[:, c:c + MXU_WIDTH], preferred_element_type=F32)
            u = jnp.dot(h, wu_ref[:, c:c + MXU_WIDTH], preferred_element_type=F32)
            if c == 0:
                u = anchor(u)
            groups.append((jax.nn.silu(g) * u).astype(BF16))
        return groups

    _three_stage_step(n_tiles, n_slices, 0.5, down_lhs, xn_ref, xf_ref, gpre_ref, gpost_ref,
                      wo_ref, o_ref, h_ref, acc_ref)


def _ffn(x, g_pre, g_post, w_in, w_out, cast_next):
    lead = ()
    t, d = x.shape
    f = w_out.shape[-2]
    tm = _pick_tile(t, 1024, BF16_SUBLANES)
    tf = _pick_tile(f, 512, LANES)
    nj = f // tf
    return _three_stage_call(
        "ffn_half_step", _ffn_kernel, x, g_pre, g_post, tm, nj,
        [(w_in, lambda chunk: _stacked_spec(lead, (d, tf), lambda r, j: (0, chunk(r, j)))),
         (w_in, lambda chunk: _stacked_spec(lead, (d, tf), lambda r, j: (0, nj + chunk(r, j)))),
         (w_out, lambda chunk: _stacked_spec(lead, (tf, d), lambda r, j: (chunk(r, j), 0)))],
        [], cast_next)


def _conv_kernel(tiles_per_seq, n_tiles, n_slices, xn_ref, xf_ref, gpre_ref, gpost_ref, wb_ref,
                 wc_ref, wu_ref, cw_ref, wo_ref, o_ref, h_ref, acc_ref, zbuf_ref, halo_ref):
    j = pl.program_id(1)
    tm = h_ref.shape[1]

    @pl.when((pl.program_id(0) == 0) & (j == 0))
    def _():
        halo_ref[...] = jnp.zeros_like(halo_ref)

    def down_lhs(h, anchor, tile):
        groups = []
        for c0 in range(0, wb_ref.shape[1], MXU_WIDTH):
            cols = slice(c0, c0 + MXU_WIDTH)
            b = jnp.dot(h, wb_ref[:, cols], preferred_element_type=F32)
            c = jnp.dot(h, wc_ref[:, cols], preferred_element_type=F32)
            u = jnp.dot(h, wu_ref[:, cols], preferred_element_type=F32)
            if c0 == 0:
                u = anchor(u)
            z = c * u

            zbuf_ref[pl.ds(SUBLANES, tm), cols] = z
            prev = halo_ref[j, :, cols]
            zbuf_ref[pl.ds(0, SUBLANES), cols] = jnp.where(tile % tiles_per_seq == 0,
                                                           jnp.zeros_like(prev), prev)
            halo_ref[j, :, cols] = z[tm - SUBLANES:, :]

            cw = cw_ref[:, cols]
            zc = (cw[0:1, :] * zbuf_ref[pl.ds(SUBLANES - 2, tm), cols]
                  + cw[1:2, :] * zbuf_ref[pl.ds(SUBLANES - 1, tm), cols]
                  + cw[2:3, :] * z)
            groups.append((b * zc).astype(BF16))
        return groups

    _three_stage_step(n_tiles, n_slices, 1.0, down_lhs, xn_ref, xf_ref, gpre_ref, gpost_ref,
                      wo_ref, o_ref, h_ref, acc_ref)


def _conv_mixer(x, seq, g_pre, g_post, w_in, conv_w, w_out, cast_next):
    lead = ()
    t, d = x.shape
    tm = _pick_tile(seq, 512, BF16_SUBLANES)
    tn = _pick_tile(d, 512, LANES)
    nj = d // tn

    def w_in_cols(part):
        return lambda chunk: _stacked_spec(lead, (d, tn), lambda r, j: (0, part * nj + chunk(r, j)))

    return _three_stage_call(
        "short_conv_mixer", functools.partial(_conv_kernel, seq // tm), x, g_pre, g_post, tm, nj,
        [(w_in, w_in_cols(0)), (w_in, w_in_cols(1)), (w_in, w_in_cols(2)),
         (conv_w, lambda chunk: _stacked_spec(lead, (CONV_WIDTH, tn), lambda r, j: (0, chunk(r, j)))),
         (w_out, lambda chunk: _stacked_spec(lead, (tn, d), lambda r, j: (chunk(r, j), 0)))],
        [pltpu.VMEM((tm + SUBLANES, tn), F32), pltpu.VMEM((nj, SUBLANES, tn), F32)], cast_next)


def _qkv_pass_col(j):
    return (2 * j) % 3


def _qkv_kernel(x_ref, gpre_ref, w_ref, cos_ref, sin_ref, o_ref, ksum_ref, h_ref):
    j = pl.program_id(1)
    tm, d = x_ref.shape
    half = ROT_DIM // 2
    nb = tm // MOBA_BLOCK

    @pl.when(j == 0)
    def _():
        h_ref[...] = _rms(x_ref[...], gpre_ref[...]).astype(BF16)

    rotary = j != 1
    cos = jnp.where(rotary, cos_ref[...], 1.0)
    sin = jnp.where(rotary, sin_ref[...], 0.0)
    lane = lax.broadcasted_iota(jnp.int32, (tm, HEAD_DIM), 1)
    pad_rows = jnp.zeros((SUBLANES - nb, HEAD_DIM), F32)
    h = h_ref[...]
    for c in range(d // MXU_WIDTH):
        r = jnp.dot(h, w_ref[:, c * MXU_WIDTH:(c + 1) * MXU_WIDTH], preferred_element_type=F32)
        for hh in range(MXU_WIDTH // HEAD_DIM):
            th = r[:, hh * HEAD_DIM:(hh + 1) * HEAD_DIM]
            partner = jnp.where(lane < half,
                                pltpu.roll(th, HEAD_DIM - half, 1),
                                pltpu.roll(th, half, 1))
            rot = th * cos + partner * sin
            cols = slice(c * MXU_WIDTH + hh * HEAD_DIM, c * MXU_WIDTH + (hh + 1) * HEAD_DIM)
            o_ref[:, cols] = rot.astype(BF16)
            sums = jnp.sum(rot.reshape(nb, MOBA_BLOCK, HEAD_DIM), axis=1)
            ksum_ref[0, :, cols] = jnp.concatenate([sums, pad_rows], axis=0) if nb < SUBLANES else sums


def _rope_tables(seq):
    half = ROT_DIM // 2
    inv_freq = ROPE_THETA ** (-jnp.arange(0, ROT_DIM, 2, dtype=F32) / ROT_DIM)
    ang = jnp.arange(seq, dtype=F32)[:, None] * inv_freq[None, :]
    cos, sin = jnp.cos(ang), jnp.sin(ang)
    pad = HEAD_DIM - ROT_DIM
    cos_t = jnp.concatenate([cos, cos, jnp.ones((seq, pad), F32)], axis=-1)
    sin_t = jnp.concatenate([-sin, sin, jnp.zeros((seq, pad), F32)], axis=-1)
    return cos_t, sin_t


def _qkv_proj(x, seq, g_pre, w_qkv):
    lead = ()
    t, d = x.shape
    tm = _pick_tile(seq, 1024, MOBA_BLOCK)
    assert tm // MOBA_BLOCK <= SUBLANES
    tiles_per_seq = seq // tm
    cos_t, sin_t = _rope_tables(seq)
    qkv, ksum = pl.pallas_call(
        _qkv_kernel,
        grid=(t // tm, 3),
        in_specs=[
            pl.BlockSpec((tm, d), lambda i, j: (i, 0)),
            pl.BlockSpec((1, d), lambda i, j: (0, 0)),
            _stacked_spec(lead, (d, d), lambda i, j: (0, _qkv_pass_col(j))),
            pl.BlockSpec((tm, HEAD_DIM), lambda i, j: (i % tiles_per_seq, 0)),
            pl.BlockSpec((tm, HEAD_DIM), lambda i, j: (i % tiles_per_seq, 0)),
        ],
        out_specs=[
            pl.BlockSpec((tm, d), lambda i, j: (i, _qkv_pass_col(j))),
            pl.BlockSpec((1, SUBLANES, d), lambda i, j: (i, 0, 0)),
        ],
        out_shape=[
            jax.ShapeDtypeStruct((t, 3 * d), BF16),
            jax.ShapeDtypeStruct((t // tm, SUBLANES, d), F32),
        ],
        scratch_shapes=[pltpu.VMEM((tm, d), BF16)],
        compiler_params=_params(("arbitrary", "arbitrary")),
        name="qkv_rope_proj",
    )(x, g_pre, w_qkv, cos_t, sin_t)
    ksum = ksum[:, :tm // MOBA_BLOCK, :].reshape(t // MOBA_BLOCK, d)
    return qkv, ksum


MOBA_HEADS_PER_STEP = 2


def _moba_kernel(q_ref, k_ref, v_ref, ksum_ref, o_ref, s_ref):
    heads = []
    for hh in range(q_ref.shape[1] // HEAD_DIM):
        cols = pl.ds(hh * HEAD_DIM, HEAD_DIM)
        heads.append((q_ref.at[:, cols], k_ref.at[:, cols], v_ref.at[:, cols], ksum_ref.at[:, cols],
                      o_ref.at[:, cols], s_ref.at[hh]))
    maxes = [_moba_scores(q, k, ksum, s) for q, k, v, ksum, o, s in heads]
    for (q, k, v, ksum, o, s), m_of in zip(heads, maxes):
        _moba_outputs(v, o, s, m_of)


def _tile_index(i, jb):
    return i * (i + 1) // 2 + jb


def _moba_scores(q_ref, k_ref, ksum_ref, s_ref):
    seq = q_ref.shape[0]
    nb = seq // MOBA_BLOCK
    blk = MOBA_BLOCK
    nt = (((1,), (1,)), ((), ()))

    kmean = (ksum_ref[...] * (1.0 / blk)).astype(BF16)
    gate = lax.dot_general(kmean, q_ref[...], nt, preferred_element_type=F32)
    n_idx = lax.broadcasted_iota(jnp.int32, (nb, seq), 0)
    q_blk = lax.broadcasted_iota(jnp.int32, (nb, seq), 1) // blk
    rank = jnp.zeros((nb, seq), jnp.int32)
    for m in range(nb - 1):
        gm = gate[m:m + 1, :]
        beats = (gm > gate) | ((gm == gate) & (m < n_idx))
        rank = rank + jnp.where(beats & (m < q_blk), 1, 0)
    sel = jnp.where((n_idx < q_blk) & (rank < MOBA_TOPK), 1.0, 0.0).astype(F32)

    key_i = lax.broadcasted_iota(jnp.int32, (blk, blk), 0)
    qry_i = lax.broadcasted_iota(jnp.int32, (blk, blk), 1)
    causal = key_i <= qry_i

    m_of = {}
    for i in reversed(range(nb)):
        qi = q_ref[i * blk:(i + 1) * blk, :]
        m_run = None
        for jb in range(i + 1):
            kj = k_ref[jb * blk:(jb + 1) * blk, :]
            s = lax.dot_general(kj, qi, nt, preferred_element_type=F32)
            if jb == i:
                s = jnp.where(causal, s, NEG)
            else:
                s = jnp.where(sel[jb:jb + 1, i * blk:(i + 1) * blk] > 0.5, s, NEG)
            s_ref[_tile_index(i, jb)] = s
            m_tile = jnp.max(s.reshape(blk // SUBLANES, SUBLANES, blk), axis=0)
            m_run = m_tile if m_run is None else jnp.maximum(m_run, m_tile)
        m_of[i] = jnp.max(m_run, axis=0, keepdims=True)
    return m_of


def _moba_outputs(v_ref, o_ref, s_ref, m_of):
    blk = MOBA_BLOCK
    tn = (((0,), (0,)), ((), ()))
    exp2_scale = HEAD_DIM ** -0.5 * LOG2_E
    for i in sorted(m_of, reverse=True):
        l_run = None
        acc = None
        for jb in range(i + 1):
            p = jnp.exp2((s_ref[_tile_index(i, jb)] - m_of[i]) * exp2_scale)
            l_tile = jnp.sum(p.reshape(blk // SUBLANES, SUBLANES, blk), axis=0)
            l_run = l_tile if l_run is None else l_run + l_tile
            vj = v_ref[jb * blk:(jb + 1) * blk, :]
            pv = lax.dot_general(vj, p.astype(BF16), tn, preferred_element_type=F32)
            acc = pv if acc is None else acc + pv
        l = jnp.sum(l_run, axis=0, keepdims=True)
        o_ref[i * blk:(i + 1) * blk, :] = (acc / l).T.astype(BF16)


def _moba_attention(qkv, ksum, bsz, seq, cast_next):
    t, d3 = qkv.shape
    d = d3 // 3
    hp = MOBA_HEADS_PER_STEP if (d // HEAD_DIM) % MOBA_HEADS_PER_STEP == 0 else 1
    nh = d // (HEAD_DIM * hp)
    nb = seq // MOBA_BLOCK
    width = hp * HEAD_DIM
    cast_arrays, cast_in, cast_out, cast_shapes = _cast_jobs(
        cast_next, bsz * nh, lambda b, h: b * nh + h)
    out, *cast = pl.pallas_call(
        functools.partial(_with_cast_jobs, _moba_kernel, 4, len(cast_arrays)),
        grid=(bsz, nh),
        in_specs=[
            pl.BlockSpec((seq, width), lambda b, h: (b, h)),
            pl.BlockSpec((seq, width), lambda b, h: (b, nh + h)),
            pl.BlockSpec((seq, width), lambda b, h: (b, 2 * nh + h)),
            pl.BlockSpec((nb, width), lambda b, h: (b, h)),
        ] + cast_in,
        out_specs=[pl.BlockSpec((seq, width), lambda b, h: (b, h))] + cast_out,
        out_shape=[jax.ShapeDtypeStruct((t, d), BF16)] + cast_shapes,
        scratch_shapes=[pltpu.VMEM((hp, nb * (nb + 1) // 2, MOBA_BLOCK, MOBA_BLOCK), F32)],
        compiler_params=_params(("arbitrary", "arbitrary")),
        name="moba_attention",
    )(qkv, qkv, qkv, ksum, *cast_arrays)
    return out, cast


def _oproj_kernel(x_ref, o_ref, w_ref, gpost_ref, out_ref):
    m = jnp.dot(o_ref[...], w_ref[...], preferred_element_type=F32)
    out_ref[...] = x_ref[...] + _rms(m, gpost_ref[...])


def _out_proj(x, o, w_o, g_post):
    lead = ()
    t, d = x.shape
    tm = _pick_tile(t, 512, SUBLANES)
    return pl.pallas_call(
        _oproj_kernel,
        grid=(t // tm,),
        in_specs=[
            pl.BlockSpec((tm, d), lambda i: (i, 0)),
            pl.BlockSpec((tm, d), lambda i: (i, 0)),
            _stacked_spec(lead, (d, d), lambda i: (0, 0)),
            pl.BlockSpec((1, d), lambda i: (0, 0)),
        ],
        out_specs=pl.BlockSpec((tm, d), lambda i: (i, 0)),
        out_shape=jax.ShapeDtypeStruct((t, d), F32),
        compiler_params=_params(("arbitrary",)),
        name="attn_out_proj",
    )(x, o, w_o, g_post)


def kernel(x, norm_g, ffn_w_in, ffn_w_out, conv_w_in, conv_w, conv_w_out, attn_w_qkv, attn_w_o):
    bsz, seq, d = x.shape
    depth = norm_g.shape[0]
    assert d % HEAD_DIM == 0 and seq % MOBA_BLOCK == 0
    assert min(MOBA_TOPK, seq // MOBA_BLOCK - 1) == MOBA_TOPK
    h = x.reshape(bsz * seq, d)

    def ffn_weights(i, half):
        return [(ffn_w_in, (i, half)), (ffn_w_out, (i, half))]

    def mixer_weights(i):
        if i % 2 == 0:
            return [(conv_w_in, (i // 2,)), (conv_w_out, (i // 2,))]
        return [(attn_w_qkv, (i // 2,)), (attn_w_o, (i // 2,))]

    w_in, w_out = ffn_w_in[0, 0].astype(BF16), ffn_w_out[0, 0].astype(BF16)
    for i in range(depth):
        g = norm_g[i][:, None, :]
        h, (wm_a, wm_b) = _ffn(h, g[0], g[1], w_in, w_out, mixer_weights(i))
        if i % 2 == 0:
            h, (w_in, w_out) = _conv_mixer(h, seq, g[2], g[3], wm_a, conv_w[i // 2], wm_b,
                                           ffn_weights(i, 1))
        else:
            qkv, ksum = _qkv_proj(h, seq, g[2], wm_a)
            o, (w_in, w_out) = _moba_attention(qkv, ksum, bsz, seq, ffn_weights(i, 1))
            h = _out_proj(h, o, wm_b, g[3])
        last = i == depth - 1
        h, nxt = _ffn(h, g[4], g[5], w_in, w_out, [] if last else ffn_weights(i + 1, 0))
        if not last:
            w_in, w_out = nxt
    return h.reshape(bsz, seq, d)
```

```python
import functools

import jax
import jax.numpy as jnp
from jax import lax
from jax.experimental import pallas as pl
from jax.experimental.pallas import tpu as pltpu

HEAD_DIM = 128
ROT_DIM = HEAD_DIM // 4
ROPE_THETA = 500000.0
MOBA_BLOCK = 256
MOBA_TOPK = 3
CONV_WIDTH = 3
RMS_EPS = 1e-6
NEG = -1e30
LOG2_E = 1.4426950408889634

LANES = 128
SUBLANES = 8
MXU_WIDTH = 256
V7X_VMEM_LIMIT_BYTES = 56 * 1024 * 1024

F32 = jnp.float32
BF16 = jnp.bfloat16


def _pick_tile(n, target, multiple):
    t = min(n, target)
    t -= t % multiple
    while t > multiple and n % t:
        t -= multiple
    assert t > 0 and n % t == 0, (n, target, multiple)
    return t


def _rms(x, g):
    ms = jnp.mean(x * x, axis=-1, keepdims=True)
    return (x * lax.rsqrt(ms + RMS_EPS)) * g


def _params(sem):
    return pltpu.CompilerParams(dimension_semantics=sem,
                                vmem_limit_bytes=V7X_VMEM_LIMIT_BYTES)


def _stacked_spec(lead, block, index_fn):
    lead = tuple(lead)
    return pl.BlockSpec((None,) * len(lead) + tuple(block), lambda *g: lead + tuple(index_fn(*g)))


BF16_SUBLANES = 16


def _slices_per_tile(tm, nj):
    n = min(nj, tm // BF16_SUBLANES)
    while tm % n or (tm // n) % BF16_SUBLANES:
        n -= 1
    return n


def _dependent_zero(*vals):
    words = []
    for v in vals:
        bits = pltpu.bitcast(v, jnp.uint32)
        words += [bits[s:s + SUBLANES, c:c + LANES]
                  for c in range(0, bits.shape[1], LANES) for s in range(0, bits.shape[0], SUBLANES)]
    while len(words) > 1:
        words = [a | b for a, b in zip(words[0::2], words[1::2])] + words[len(words) & ~1:]
    return (words[0] >> 16) >> 16


def _three_stage_step(n_tiles, n_slices, res_scale, down_lhs, xn_ref, xf_ref, gpre_ref, gpost_ref,
                      wo_ref, o_ref, h_ref, acc_ref):
    r = pl.program_id(0)
    j = pl.program_id(1)
    sr = xn_ref.shape[0]
    rows = pl.ds(pl.multiple_of(jnp.minimum(j, n_slices - 1) * sr, sr), sr)
    norm_slot = r % 2
    mm_slot = 1 - norm_slot

    @pl.when((r == 0) & (j == 0))
    def _():
        acc_ref[...] = jnp.zeros_like(acc_ref)

    def finalize_slice():
        out = xf_ref[...] + res_scale * _rms(acc_ref[norm_slot, rows, :], gpost_ref[...])
        o_ref[...] = out
        return out

    def norm_slice():
        hn = _rms(xn_ref[...], gpre_ref[...])
        h_ref[norm_slot, rows, :] = hn.astype(BF16)
        return hn

    def anchor(v):
        zero = _dependent_zero(finalize_slice(), norm_slice())
        zero = jnp.tile(zero, (v.shape[0] // SUBLANES, v.shape[1] // LANES))
        return pltpu.bitcast(pltpu.bitcast(v, jnp.uint32) | zero, F32)

    has_matmul = (r >= 1) & (r <= n_tiles)

    @pl.when(has_matmul)
    def _():
        k0 = 0
        for a in down_lhs(h_ref[mm_slot], anchor, r - 1):
            kc = a.shape[1]
            prev = jnp.where(j == 0, 0.0, acc_ref[mm_slot]) if k0 == 0 else acc_ref[mm_slot]
            acc_ref[mm_slot] = jnp.dot(a, wo_ref[k0:k0 + kc, :], preferred_element_type=F32) + prev
            k0 += kc

    @pl.when(jnp.logical_not(has_matmul))
    def _():
        finalize_slice()
        norm_slice()


def _cast_piece_rows(rows, steps):
    for pr in range(BF16_SUBLANES, rows + 1, BF16_SUBLANES):
        if rows % pr == 0 and rows // pr <= steps:
            return pr
    raise ValueError((rows, steps))


def _cast_jobs(cast_next, steps, step_index):
    arrays, in_specs, out_specs, out_shapes = [], [], [], []
    for w, lead in cast_next:
        rows, cols = w.shape[-2:]
        pr = _cast_piece_rows(rows, steps)

        def piece(*g, last=rows // pr - 1):
            return (jnp.minimum(step_index(*g), last), 0)

        arrays.append(w)
        in_specs.append(_stacked_spec(lead, (pr, cols), piece))
        out_specs.append(pl.BlockSpec((pr, cols), piece))
        out_shapes.append(jax.ShapeDtypeStruct((rows, cols), BF16))
    return arrays, in_specs, out_specs, out_shapes


def _with_cast_jobs(body, n_in, n_cast, *refs):
    for src, dst in zip(refs[n_in:n_in + n_cast], refs[n_in + n_cast + 1:n_in + 2 * n_cast + 1]):
        dst[...] = src[...].astype(BF16)
    body(*refs[:n_in], refs[n_in + n_cast], *refs[n_in + 2 * n_cast + 1:])


def _three_stage_call(name, body, x, g_pre, g_post, tm, nj, chunked_inputs, scratch_shapes,
                      cast_next):
    t, d = x.shape
    n_tiles = t // tm
    n_slices = _slices_per_tile(tm, nj)
    sr = tm // n_slices
    cast_arrays, cast_in, cast_out, cast_shapes = _cast_jobs(
        cast_next, (n_tiles + 2) * nj, lambda r, j: r * nj + j)

    def norm_slice(r, j):
        return (jnp.minimum(r, n_tiles - 1) * n_slices + jnp.minimum(j, n_slices - 1), 0)

    def final_slice(r, j):
        return (jnp.where(r < 2, 0, (r - 2) * n_slices + jnp.minimum(j, n_slices - 1)), 0)

    def chunk(r, j):
        return jnp.where((r >= 1) & (r <= n_tiles), j, 0)

    out, *cast = pl.pallas_call(
        functools.partial(_with_cast_jobs, functools.partial(body, n_tiles, n_slices),
                          4 + len(chunked_inputs), len(cast_arrays)),
        grid=(n_tiles + 2, nj),
        in_specs=[
            pl.BlockSpec((sr, d), norm_slice),
            pl.BlockSpec((sr, d), final_slice),
            pl.BlockSpec((1, d), lambda r, j: (0, 0)),
            pl.BlockSpec((1, d), lambda r, j: (0, 0)),
        ] + [spec_fn(chunk) for _, spec_fn in chunked_inputs] + cast_in,
        out_specs=[pl.BlockSpec((sr, d), final_slice)] + cast_out,
        out_shape=[jax.ShapeDtypeStruct((t, d), F32)] + cast_shapes,
        scratch_shapes=[pltpu.VMEM((2, tm, d), BF16), pltpu.VMEM((2, tm, d), F32)] + scratch_shapes,
        compiler_params=_params(("arbitrary", "arbitrary")),
        name=name,
    )(x, x, g_pre, g_post, *[a for a, _ in chunked_inputs], *cast_arrays)
    return out, cast


def _ffn_kernel(n_tiles, n_slices, xn_ref, xf_ref, gpre_ref, gpost_ref, wg_ref, wu_ref, wo_ref,
                o_ref, h_ref, acc_ref):
    def down_lhs(h, anchor, tile):
        groups = []
        for c in range(0, wg_ref.shape[1], MXU_WIDTH):
            g = jnp.dot(h, wg_ref[:, c:c + MXU_WIDTH], preferred_element_type=F32)
            u = jnp.dot(h, wu_ref[:, c:c + MXU_WIDTH], preferred_element_type=F32)
            if c == 0:
                u = anchor(u)
            groups.append((jax.nn.silu(g) * u).astype(BF16))
        return groups

    _three_stage_step(n_tiles, n_slices, 0.5, down_lhs, xn_ref, xf_ref, gpre_ref, gpost_ref,
                      wo_ref, o_ref, h_ref, acc_ref)


def _ffn(x, g_pre, g_post, w_in, w_out, cast_next):
    lead = ()
    t, d = x.shape
    f = w_out.shape[-2]
    tm = _pick_tile(t, 1024, BF16_SUBLANES)
    tf = _pick_tile(f, 512, LANES)
    nj = f // tf
    return _three_stage_call(
        "ffn_half_step", _ffn_kernel, x, g_pre, g_post, tm, nj,
        [(w_in, lambda chunk: _stacked_spec(lead, (d, tf), lambda r, j: (0, chunk(r, j)))),
         (w_in, lambda chunk: _stacked_spec(lead, (d, tf), lambda r, j: (0, nj + chunk(r, j)))),
         (w_out, lambda chunk: _stacked_spec(lead, (tf, d), lambda r, j: (chunk(r, j), 0)))],
        [], cast_next)


def _conv_kernel(tiles_per_seq, n_tiles, n_slices, xn_ref, xf_ref, gpre_ref, gpost_ref, wb_ref,
                 wc_ref, wu_ref, cw_ref, wo_ref, o_ref, h_ref, acc_ref, zbuf_ref, halo_ref):
    j = pl.program_id(1)
    tm = h_ref.shape[1]

    @pl.when((pl.program_id(0) == 0) & (j == 0))
    def _():
        halo_ref[...] = jnp.zeros_like(halo_ref)

    def down_lhs(h, anchor, tile):
        groups = []
        for c0 in range(0, wb_ref.shape[1], MXU_WIDTH):
            cols = slice(c0, c0 + MXU_WIDTH)
            b = jnp.dot(h, wb_ref[:, cols], preferred_element_type=F32)
            c = jnp.dot(h, wc_ref[:, cols], preferred_element_type=F32)
            u = jnp.dot(h, wu_ref[:, cols], preferred_element_type=F32)
            if c0 == 0:
                u = anchor(u)
            z = c * u

            zbuf_ref[pl.ds(SUBLANES, tm), cols] = z
            prev = halo_ref[j, :, cols]
            zbuf_ref[pl.ds(0, SUBLANES), cols] = jnp.where(tile % tiles_per_seq == 0,
                                                           jnp.zeros_like(prev), prev)
            halo_ref[j, :, cols] = z[tm - SUBLANES:, :]

            cw = cw_ref[:, cols]
            zc = (cw[0:1, :] * zbuf_ref[pl.ds(SUBLANES - 2, tm), cols]
                  + cw[1:2, :] * zbuf_ref[pl.ds(SUBLANES - 1, tm), cols]
                  + cw[2:3, :] * z)
            groups.append((b * zc).astype(BF16))
        return groups

    _three_stage_step(n_tiles, n_slices, 1.0, down_lhs, xn_ref, xf_ref, gpre_ref, gpost_ref,
                      wo_ref, o_ref, h_ref, acc_ref)


def _conv_mixer(x, seq, g_pre, g_post, w_in, conv_w, w_out, cast_next):
    lead = ()
    t, d = x.shape
    tm = _pick_tile(seq, 512, BF16_SUBLANES)
    tn = _pick_tile(d, 512, LANES)
    nj = d // tn

    def w_in_cols(part):
        return lambda chunk: _stacked_spec(lead, (d, tn), lambda r, j: (0, part * nj + chunk(r, j)))

    return _three_stage_call(
        "short_conv_mixer", functools.partial(_conv_kernel, seq // tm), x, g_pre, g_post, tm, nj,
        [(w_in, w_in_cols(0)), (w_in, w_in_cols(1)), (w_in, w_in_cols(2)),
         (conv_w, lambda chunk: _stacked_spec(lead, (CONV_WIDTH, tn), lambda r, j: (0, chunk(r, j)))),
         (w_out, lambda chunk: _stacked_spec(lead, (tn, d), lambda r, j: (chunk(r, j), 0)))],
        [pltpu.VMEM((tm + SUBLANES, tn), F32), pltpu.VMEM((nj, SUBLANES, tn), F32)], cast_next)


def _qkv_pass_col(j):
    return (2 * j) % 3


def _qkv_kernel(x_ref, gpre_ref, w_ref, cos_ref, sin_ref, o_ref, ksum_ref, h_ref):
    j = pl.program_id(1)
    tm, d = x_ref.shape
    half = ROT_DIM // 2
    nb = tm // MOBA_BLOCK

    @pl.when(j == 0)
    def _():
        h_ref[...] = _rms(x_ref[...], gpre_ref[...]).astype(BF16)

    rotary = j != 1
    cos = jnp.where(rotary, cos_ref[...], 1.0)
    sin = jnp.where(rotary, sin_ref[...], 0.0)
    lane = lax.broadcasted_iota(jnp.int32, (tm, HEAD_DIM), 1)
    pad_rows = jnp.zeros((SUBLANES - nb, HEAD_DIM), F32)
    h = h_ref[...]
    for c in range(d // MXU_WIDTH):
        r = jnp.dot(h, w_ref[:, c * MXU_WIDTH:(c + 1) * MXU_WIDTH], preferred_element_type=F32)
        for hh in range(MXU_WIDTH // HEAD_DIM):
            th = r[:, hh * HEAD_DIM:(hh + 1) * HEAD_DIM]
            partner = jnp.where(lane < half,
                                pltpu.roll(th, HEAD_DIM - half, 1),
                                pltpu.roll(th, half, 1))
            rot = th * cos + partner * sin
            cols = slice(c * MXU_WIDTH + hh * HEAD_DIM, c * MXU_WIDTH + (hh + 1) * HEAD_DIM)
            o_ref[:, cols] = rot.astype(BF16)
            sums = jnp.sum(rot.reshape(nb, MOBA_BLOCK, HEAD_DIM), axis=1)
            ksum_ref[0, :, cols] = jnp.concatenate([sums, pad_rows], axis=0) if nb < SUBLANES else sums


def _rope_tables(seq):
    half = ROT_DIM // 2
    inv_freq = ROPE_THETA ** (-jnp.arange(0, ROT_DIM, 2, dtype=F32) / ROT_DIM)
    ang = jnp.arange(seq, dtype=F32)[:, None] * inv_freq[None, :]
    cos, sin = jnp.cos(ang), jnp.sin(ang)
    pad = HEAD_DIM - ROT_DIM
    cos_t = jnp.concatenate([cos, cos, jnp.ones((seq, pad), F32)], axis=-1)
    sin_t = jnp.concatenate([-sin, sin, jnp.zeros((seq, pad), F32)], axis=-1)
    return cos_t, sin_t


def _qkv_proj(x, seq, g_pre, w_qkv):
    lead = ()
    t, d = x.shape
    tm = _pick_tile(seq, 1024, MOBA_BLOCK)
    assert tm // MOBA_BLOCK <= SUBLANES
    tiles_per_seq = seq // tm
    cos_t, sin_t = _rope_tables(seq)
    qkv, ksum = pl.pallas_call(
        _qkv_kernel,
        grid=(t // tm, 3),
        in_specs=[
            pl.BlockSpec((tm, d), lambda i, j: (i, 0)),
            pl.BlockSpec((1, d), lambda i, j: (0, 0)),
            _stacked_spec(lead, (d, d), lambda i, j: (0, _qkv_pass_col(j))),
            pl.BlockSpec((tm, HEAD_DIM), lambda i, j: (i % tiles_per_seq, 0)),
            pl.BlockSpec((tm, HEAD_DIM), lambda i, j: (i % tiles_per_seq, 0)),
        ],
        out_specs=[
            pl.BlockSpec((tm, d), lambda i, j: (i, _qkv_pass_col(j))),
            pl.BlockSpec((1, SUBLANES, d), lambda i, j: (i, 0, 0)),
        ],
        out_shape=[
            jax.ShapeDtypeStruct((t, 3 * d), BF16),
            jax.ShapeDtypeStruct((t // tm, SUBLANES, d), F32),
        ],
        scratch_shapes=[pltpu.VMEM((tm, d), BF16)],
        compiler_params=_params(("arbitrary", "arbitrary")),
        name="qkv_rope_proj",
    )(x, g_pre, w_qkv, cos_t, sin_t)
    ksum = ksum[:, :tm // MOBA_BLOCK, :].reshape(t // MOBA_BLOCK, d)
    return qkv, ksum


MOBA_HEADS_PER_STEP = 2


def _moba_kernel(q_ref, k_ref, v_ref, ksum_ref, o_ref, s_ref):
    heads = []
    for hh in range(q_ref.shape[1] // HEAD_DIM):
        cols = pl.ds(hh * HEAD_DIM, HEAD_DIM)
        heads.append((q_ref.at[:, cols], k_ref.at[:, cols], v_ref.at[:, cols], ksum_ref.at[:, cols],
                      o_ref.at[:, cols], s_ref.at[hh]))
    maxes = [_moba_scores(q, k, ksum, s) for q, k, v, ksum, o, s in heads]
    for (q, k, v, ksum, o, s), m_of in zip(heads, maxes):
        _moba_outputs(v, o, s, m_of)


def _tile_index(i, jb):
    return i * (i + 1) // 2 + jb


def _moba_scores(q_ref, k_ref, ksum_ref, s_ref):
    seq = q_ref.shape[0]
    nb = seq // MOBA_BLOCK
    blk = MOBA_BLOCK
    nt = (((1,), (1,)), ((), ()))

    kmean = (ksum_ref[...] * (1.0 / blk)).astype(BF16)
    gate = lax.dot_general(kmean, q_ref[...], nt, preferred_element_type=F32)
    n_idx = lax.broadcasted_iota(jnp.int32, (nb, seq), 0)
    q_blk = lax.broadcasted_iota(jnp.int32, (nb, seq), 1) // blk
    rank = jnp.zeros((nb, seq), jnp.int32)
    for m in range(nb - 1):
        gm = gate[m:m + 1, :]
        beats = (gm > gate) | ((gm == gate) & (m < n_idx))
        rank = rank + jnp.where(beats & (m < q_blk), 1, 0)
    sel = jnp.where((n_idx < q_blk) & (rank < MOBA_TOPK), 1.0, 0.0).astype(F32)

    key_i = lax.broadcasted_iota(jnp.int32, (blk, blk), 0)
    qry_i = lax.broadcasted_iota(jnp.int32, (blk, blk), 1)
    causal = key_i <= qry_i

    m_of = {}
    for i in reversed(range(nb)):
        qi = q_ref[i * blk:(i + 1) * blk, :]
        m_run = None
        for jb in range(i + 1):
            kj = k_ref[jb * blk:(jb + 1) * blk, :]
            s = lax.dot_general(kj, qi, nt, preferred_element_type=F32)
            if jb == i:
                s = jnp.where(causal, s, NEG)
            elif i > MOBA_TOPK:
                s = jnp.where(sel[jb:jb + 1, i * blk:(i + 1) * blk] > 0.5, s, NEG)
            s_ref[_tile_index(i, jb)] = s
            m_tile = jnp.max(s.reshape(blk // SUBLANES, SUBLANES, blk), axis=0)
            m_run = m_tile if m_run is None else jnp.maximum(m_run, m_tile)
        m_of[i] = jnp.max(m_run, axis=0, keepdims=True)
    return m_of


def _moba_outputs(v_ref, o_ref, s_ref, m_of):
    blk = MOBA_BLOCK
    tn = (((0,), (0,)), ((), ()))
    exp2_scale = HEAD_DIM ** -0.5 * LOG2_E
    for i in sorted(m_of, reverse=True):
        l_run = None
        acc = None
        for jb in range(i + 1):
            p = jnp.exp2((s_ref[_tile_index(i, jb)] - m_of[i]) * exp2_scale)
            l_tile = jnp.sum(p.reshape(blk // SUBLANES, SUBLANES, blk), axis=0)
            l_run = l_tile if l_run is None else l_run + l_tile
            vj = v_ref[jb * blk:(jb + 1) * blk, :]
            pv = lax.dot_general(vj, p.astype(BF16), tn, preferred_element_type=F32)
            acc = pv if acc is None else acc + pv
        l = jnp.sum(l_run, axis=0, keepdims=True)
        o_ref[i * blk:(i + 1) * blk, :] = (acc / l).T.astype(BF16)


def _moba_attention(qkv, ksum, bsz, seq, cast_next):
    t, d3 = qkv.shape
    d = d3 // 3
    hp = MOBA_HEADS_PER_STEP if (d // HEAD_DIM) % MOBA_HEADS_PER_STEP == 0 else 1
    nh = d // (HEAD_DIM * hp)
    nb = seq // MOBA_BLOCK
    width = hp * HEAD_DIM
    cast_arrays, cast_in, cast_out, cast_shapes = _cast_jobs(
        cast_next, bsz * nh, lambda b, h: b * nh + h)
    out, *cast = pl.pallas_call(
        functools.partial(_with_cast_jobs, _moba_kernel, 4, len(cast_arrays)),
        grid=(bsz, nh),
        in_specs=[
            pl.BlockSpec((seq, width), lambda b, h: (b, h)),
            pl.BlockSpec((seq, width), lambda b, h: (b, nh + h)),
            pl.BlockSpec((seq, width), lambda b, h: (b, 2 * nh + h)),
            pl.BlockSpec((nb, width), lambda b, h: (b, h)),
        ] + cast_in,
        out_specs=[pl.BlockSpec((seq, width), lambda b, h: (b, h))] + cast_out,
        out_shape=[jax.ShapeDtypeStruct((t, d), BF16)] + cast_shapes,
        scratch_shapes=[pltpu.VMEM((hp, nb * (nb + 1) // 2, MOBA_BLOCK, MOBA_BLOCK), F32)],
        compiler_params=_params(("arbitrary", "arbitrary")),
        name="moba_attention",
    )(qkv, qkv, qkv, ksum, *cast_arrays)
    return out, cast


def _oproj_kernel(x_ref, o_ref, w_ref, gpost_ref, out_ref):
    m = jnp.dot(o_ref[...], w_ref[...], preferred_element_type=F32)
    out_ref[...] = x_ref[...] + _rms(m, gpost_ref[...])


def _out_proj(x, o, w_o, g_post):
    lead = ()
    t, d = x.shape
    tm = _pick_tile(t, 512, SUBLANES)
    return pl.pallas_call(
        _oproj_kernel,
        grid=(t // tm,),
        in_specs=[
            pl.BlockSpec((tm, d), lambda i: (i, 0)),
            pl.BlockSpec((tm, d), lambda i: (i, 0)),
            _stacked_spec(lead, (d, d), lambda i: (0, 0)),
            pl.BlockSpec((1, d), lambda i: (0, 0)),
        ],
        out_specs=pl.BlockSpec((tm, d), lambda i: (i, 0)),
        out_shape=jax.ShapeDtypeStruct((t, d), F32),
        compiler_params=_params(("arbitrary",)),
        name="attn_out_proj",
    )(x, o, w_o, g_post)


def kernel(x, norm_g, ffn_w_in, ffn_w_out, conv_w_in, conv_w, conv_w_out, attn_w_qkv, attn_w_o):
    bsz, seq, d = x.shape
    depth = norm_g.shape[0]
    assert d % HEAD_DIM == 0 and seq % MOBA_BLOCK == 0
    assert min(MOBA_TOPK, seq // MOBA_BLOCK - 1) == MOBA_TOPK
    h = x.reshape(bsz * seq, d)

    def ffn_weights(i, half):
        return [(ffn_w_in, (i, half)), (ffn_w_out, (i, half))]

    def mixer_weights(i):
        if i % 2 == 0:
            return [(conv_w_in, (i // 2,)), (conv_w_out, (i // 2,))]
        return [(attn_w_qkv, (i // 2,)), (attn_w_o, (i // 2,))]

    w_in, w_out = ffn_w_in[0, 0].astype(BF16), ffn_w_out[0, 0].astype(BF16)
    for i in range(depth):
        g = norm_g[i][:, None, :]
        h, (wm_a, wm_b) = _ffn(h, g[0], g[1], w_in, w_out, mixer_weights(i))
        if i % 2 == 0:
            h, (w_in, w_out) = _conv_mixer(h, seq, g[2], g[3], wm_a, conv_w[i // 2], wm_b,
                                           ffn_weights(i, 1))
        else:
            qkv, ksum = _qkv_proj(h, seq, g[2], wm_a)
            o, (w_in, w_out) = _moba_attention(qkv, ksum, bsz, seq, ffn_weights(i, 1))
            h = _out_proj(h, o, wm_b, g[3])
        last = i == depth - 1
        h, nxt = _ffn(h, g[4], g[5], w_in, w_out, [] if last else ffn_weights(i + 1, 0))
        if not last:
            w_in, w_out = nxt
    return h.reshape(bsz, seq, d)
```

```python
import functools

import jax
import jax.numpy as jnp
from jax import lax
from jax.experimental import pallas as pl
from jax.experimental.pallas import tpu as pltpu

HEAD_DIM = 128
ROT_DIM = HEAD_DIM // 4
ROPE_THETA = 500000.0
MOBA_BLOCK = 256
MOBA_TOPK = 3
CONV_WIDTH = 3
RMS_EPS = 1e-6
NEG = -1e30
LOG2_E = 1.4426950408889634

LANES = 128
SUBLANES = 8
MXU_WIDTH = 256
V7X_VMEM_LIMIT_BYTES = 56 * 1024 * 1024

F32 = jnp.float32
BF16 = jnp.bfloat16


def _pick_tile(n, target, multiple):
    t = min(n, target)
    t -= t % multiple
    while t > multiple and n % t:
        t -= multiple
    assert t > 0 and n % t == 0, (n, target, multiple)
    return t


def _rms(x, g):
    ms = jnp.mean(x * x, axis=-1, keepdims=True)
    return (x * lax.rsqrt(ms + RMS_EPS)) * g


def _params(sem):
    return pltpu.CompilerParams(dimension_semantics=sem,
                                vmem_limit_bytes=V7X_VMEM_LIMIT_BYTES)


def _stacked_spec(lead, block, index_fn):
    lead = tuple(lead)
    return pl.BlockSpec((None,) * len(lead) + tuple(block), lambda *g: lead + tuple(index_fn(*g)))


BF16_SUBLANES = 16


def _slices_per_tile(tm, nj):
    n = min(nj, tm // BF16_SUBLANES)
    while tm % n or (tm // n) % BF16_SUBLANES:
        n -= 1
    return n


def _dependent_zero(*vals):
    words = []
    for v in vals:
        bits = pltpu.bitcast(v, jnp.uint32)
        words += [bits[s:s + SUBLANES, c:c + LANES]
                  for c in range(0, bits.shape[1], LANES) for s in range(0, bits.shape[0], SUBLANES)]
    while len(words) > 1:
        words = [a | b for a, b in zip(words[0::2], words[1::2])] + words[len(words) & ~1:]
    return (words[0] >> 16) >> 16


def _three_stage_step(n_tiles, n_slices, res_scale, down_lhs, stage_refs, wo_ref, outs, h_ref,
                      acc_ref, cast_pieces):
    xn_ref, xf_ref, gpre_ref, gpost_ref = stage_refs[:4]
    o_ref = outs[0]
    r = pl.program_id(0)
    j = pl.program_id(1)
    sr = xn_ref.shape[0]
    rows = pl.ds(pl.multiple_of(jnp.minimum(j, n_slices - 1) * sr, sr), sr)
    norm_slot = r % 2
    mm_slot = 1 - norm_slot

    @pl.when((r == 0) & (j == 0))
    def _():
        acc_ref[...] = jnp.zeros_like(acc_ref)

    def finalize_slice():
        out = xf_ref[...] + res_scale * _rms(acc_ref[norm_slot, rows, :], gpost_ref[...])
        o_ref[...] = out
        if len(outs) == 1:
            return [out]
        nxt = _rms(out, stage_refs[4][...])
        outs[1][...] = nxt.astype(BF16)
        return [out, nxt]

    def norm_slice():
        hn = _rms(xn_ref[...], gpre_ref[...])
        h_ref[norm_slot, rows, :] = hn.astype(BF16)
        return hn

    def anchor(v):
        zero = _dependent_zero(*finalize_slice(), norm_slice(), *cast_pieces())
        zero = jnp.tile(zero, (v.shape[0] // SUBLANES, v.shape[1] // LANES))
        return pltpu.bitcast(pltpu.bitcast(v, jnp.uint32) | zero, F32)

    has_matmul = (r >= 1) & (r <= n_tiles)

    @pl.when(has_matmul)
    def _():
        k0 = 0
        for a in down_lhs(h_ref[mm_slot], anchor, r - 1):
            kc = a.shape[1]
            prev = jnp.where(j == 0, 0.0, acc_ref[mm_slot]) if k0 == 0 else acc_ref[mm_slot]
            acc_ref[mm_slot] = jnp.dot(a, wo_ref[k0:k0 + kc, :], preferred_element_type=F32) + prev
            k0 += kc

    @pl.when(jnp.logical_not(has_matmul))
    def _():
        finalize_slice()
        norm_slice()
        cast_pieces()


def _cast_piece_rows(rows, steps):
    for pr in range(BF16_SUBLANES, rows + 1, BF16_SUBLANES):
        if rows % pr == 0 and rows // pr <= steps:
            return pr
    raise ValueError((rows, steps))


def _cast_jobs(cast_next, steps, step_index):
    arrays, in_specs, out_specs, out_shapes = [], [], [], []
    for w, lead in cast_next:
        rows, cols = w.shape[-2:]
        pr = _cast_piece_rows(rows, steps)

        def piece(*g, last=rows // pr - 1):
            return (jnp.minimum(step_index(*g), last), 0)

        arrays.append(w)
        in_specs.append(_stacked_spec(lead, (pr, cols), piece))
        out_specs.append(pl.BlockSpec((pr, cols), piece))
        out_shapes.append(jax.ShapeDtypeStruct((rows, cols), BF16))
    return arrays, in_specs, out_specs, out_shapes


def _with_cast_jobs(body, n_in, n_out, n_cast, *refs):
    o0 = n_in + n_cast

    def cast_pieces():
        vals = [src[...] for src in refs[n_in:o0]]
        for v, dst in zip(vals, refs[o0 + n_out:o0 + n_out + n_cast]):
            dst[...] = v.astype(BF16)
        return vals

    body(refs[:n_in], refs[o0:o0 + n_out], refs[o0 + n_out + n_cast:], cast_pieces)


def _three_stage_call(name, body, x, g_pre, g_post, g_next, tm, nj, chunked_inputs,
                      scratch_shapes, cast_next):
    t, d = x.shape
    n_tiles = t // tm
    n_slices = _slices_per_tile(tm, nj)
    sr = tm // n_slices
    cast_arrays, cast_in, cast_out, cast_shapes = _cast_jobs(
        cast_next, (n_tiles + 2) * nj, lambda r, j: r * nj + j)
    gains = [g_pre, g_post] + ([] if g_next is None else [g_next])
    n_out = 1 if g_next is None else 2

    def norm_slice(r, j):
        return (jnp.minimum(r, n_tiles - 1) * n_slices + jnp.minimum(j, n_slices - 1), 0)

    def final_slice(r, j):
        return (jnp.where(r < 2, 0, (r - 2) * n_slices + jnp.minimum(j, n_slices - 1)), 0)

    def chunk(r, j):
        return jnp.where((r >= 1) & (r <= n_tiles), j, 0)

    results = pl.pallas_call(
        functools.partial(_with_cast_jobs,
                          functools.partial(body, n_tiles, n_slices, 2 + len(gains)),
                          2 + len(gains) + len(chunked_inputs), n_out, len(cast_arrays)),
        grid=(n_tiles + 2, nj),
        in_specs=[pl.BlockSpec((sr, d), norm_slice), pl.BlockSpec((sr, d), final_slice)]
        + [pl.BlockSpec((1, d), lambda r, j: (0, 0)) for _ in gains]
        + [spec_fn(chunk) for _, spec_fn in chunked_inputs] + cast_in,
        out_specs=[pl.BlockSpec((sr, d), final_slice)] * n_out + cast_out,
        out_shape=[jax.ShapeDtypeStruct((t, d), F32)]
        + [jax.ShapeDtypeStruct((t, d), BF16)] * (n_out - 1) + cast_shapes,
        scratch_shapes=[pltpu.VMEM((2, tm, d), BF16), pltpu.VMEM((2, tm, d), F32)] + scratch_shapes,
        compiler_params=_params(("arbitrary", "arbitrary")),
        name=name,
    )(x, x, *gains, *[a for a, _ in chunked_inputs], *cast_arrays)
    return results[0], (results[1] if n_out == 2 else None), results[n_out:]


def _ffn_kernel(n_tiles, n_slices, n_stage, ins, outs, scratch, cast_pieces):
    wg_ref, wu_ref, wo_ref = ins[n_stage:]
    h_ref, acc_ref = scratch

    def down_lhs(h, anchor, tile):
        groups = []
        for c in range(0, wg_ref.shape[1], MXU_WIDTH):
            g = jnp.dot(h, wg_ref[:, c:c + MXU_WIDTH], preferred_element_type=F32)
            u = jnp.dot(h, wu_ref[:, c:c + MXU_WIDTH], preferred_element_type=F32)
            if c == 0:
                u = anchor(u)
            groups.append((jax.nn.silu(g) * u).astype(BF16))
        return groups

    _three_stage_step(n_tiles, n_slices, 0.5, down_lhs, ins[:n_stage], wo_ref, outs, h_ref,
                      acc_ref, cast_pieces)


def _ffn(x, g_pre, g_post, w_in, w_out, cast_next, g_next=None):
    t, d = x.shape
    f = w_out.shape[-2]
    tm = _pick_tile(t, 1024, BF16_SUBLANES)
    tf = _pick_tile(f, 512, LANES)
    nj = f // tf
    return _three_stage_call(
        "ffn_half_step", _ffn_kernel, x, g_pre, g_post, g_next, tm, nj,
        [(w_in, lambda chunk: pl.BlockSpec((d, tf), lambda r, j: (0, chunk(r, j)))),
         (w_in, lambda chunk: pl.BlockSpec((d, tf), lambda r, j: (0, nj + chunk(r, j)))),
         (w_out, lambda chunk: pl.BlockSpec((tf, d), lambda r, j: (chunk(r, j), 0)))],
        [], cast_next)


def _conv_kernel(tiles_per_seq, n_tiles, n_slices, n_stage, ins, outs, scratch, cast_pieces):
    wb_ref, wc_ref, wu_ref, cw_ref, wo_ref = ins[n_stage:]
    h_ref, acc_ref, zbuf_ref, halo_ref = scratch
    j = pl.program_id(1)
    tm = h_ref.shape[1]

    @pl.when((pl.program_id(0) == 0) & (j == 0))
    def _():
        halo_ref[...] = jnp.zeros_like(halo_ref)

    def down_lhs(h, anchor, tile):
        groups = []
        for c0 in range(0, wb_ref.shape[1], MXU_WIDTH):
            cols = slice(c0, c0 + MXU_WIDTH)
            b = jnp.dot(h, wb_ref[:, cols], preferred_element_type=F32)
            c = jnp.dot(h, wc_ref[:, cols], preferred_element_type=F32)
            u = jnp.dot(h, wu_ref[:, cols], preferred_element_type=F32)
            if c0 == 0:
                u = anchor(u)
            z = c * u

            zbuf_ref[pl.ds(SUBLANES, tm), cols] = z
            prev = halo_ref[j, :, cols]
            zbuf_ref[pl.ds(0, SUBLANES), cols] = jnp.where(tile % tiles_per_seq == 0,
                                                           jnp.zeros_like(prev), prev)
            halo_ref[j, :, cols] = z[tm - SUBLANES:, :]

            cw = cw_ref[:, cols]
            zc = (cw[0:1, :] * zbuf_ref[pl.ds(SUBLANES - 2, tm), cols]
                  + cw[1:2, :] * zbuf_ref[pl.ds(SUBLANES - 1, tm), cols]
                  + cw[2:3, :] * z)
            groups.append((b * zc).astype(BF16))
        return groups

    _three_stage_step(n_tiles, n_slices, 1.0, down_lhs, ins[:n_stage], wo_ref, outs, h_ref,
                      acc_ref, cast_pieces)


def _conv_mixer(x, seq, g_pre, g_post, w_in, conv_w, w_out, cast_next):
    t, d = x.shape
    tm = _pick_tile(seq, 512, BF16_SUBLANES)
    tn = _pick_tile(d, 512, LANES)
    nj = d // tn

    def w_in_cols(part):
        return lambda chunk: pl.BlockSpec((d, tn), lambda r, j: (0, part * nj + chunk(r, j)))

    return _three_stage_call(
        "short_conv_mixer", functools.partial(_conv_kernel, seq // tm), x, g_pre, g_post, None,
        tm, nj,
        [(w_in, w_in_cols(0)), (w_in, w_in_cols(1)), (w_in, w_in_cols(2)),
         (conv_w, lambda chunk: pl.BlockSpec((CONV_WIDTH, tn), lambda r, j: (0, chunk(r, j)))),
         (w_out, lambda chunk: pl.BlockSpec((tn, d), lambda r, j: (chunk(r, j), 0)))],
        [pltpu.VMEM((tm + SUBLANES, tn), F32), pltpu.VMEM((nj, SUBLANES, tn), F32)], cast_next)


def _qkv_pass_col(j):
    return (2 * j) % 3


def _qkv_kernel(h_ref, w_ref, cos_ref, sin_ref, o_ref, ksum_ref):
    j = pl.program_id(1)
    tm, d = h_ref.shape
    half = ROT_DIM // 2
    nb = tm // MOBA_BLOCK

    rotary = j != 1
    cos = jnp.where(rotary, cos_ref[...], 1.0)
    sin = jnp.where(rotary, sin_ref[...], 0.0)
    lane = lax.broadcasted_iota(jnp.int32, (tm, HEAD_DIM), 1)
    pad_rows = jnp.zeros((SUBLANES - nb, HEAD_DIM), F32)
    h = h_ref[...]
    for c in range(d // MXU_WIDTH):
        r = jnp.dot(h, w_ref[:, c * MXU_WIDTH:(c + 1) * MXU_WIDTH], preferred_element_type=F32)
        for hh in range(MXU_WIDTH // HEAD_DIM):
            th = r[:, hh * HEAD_DIM:(hh + 1) * HEAD_DIM]
            partner = jnp.where(lane < half,
                                pltpu.roll(th, HEAD_DIM - half, 1),
                                pltpu.roll(th, half, 1))
            rot = th * cos + partner * sin
            cols = slice(c * MXU_WIDTH + hh * HEAD_DIM, c * MXU_WIDTH + (hh + 1) * HEAD_DIM)
            o_ref[:, cols] = rot.astype(BF16)
            sums = jnp.sum(rot.reshape(nb, MOBA_BLOCK, HEAD_DIM), axis=1)
            ksum_ref[0, :, cols] = jnp.concatenate([sums, pad_rows], axis=0) if nb < SUBLANES else sums


def _rope_tables(seq):
    half = ROT_DIM // 2
    inv_freq = ROPE_THETA ** (-jnp.arange(0, ROT_DIM, 2, dtype=F32) / ROT_DIM)
    ang = jnp.arange(seq, dtype=F32)[:, None] * inv_freq[None, :]
    cos, sin = jnp.cos(ang), jnp.sin(ang)
    pad = HEAD_DIM - ROT_DIM
    cos_t = jnp.concatenate([cos, cos, jnp.ones((seq, pad), F32)], axis=-1)
    sin_t = jnp.concatenate([-sin, sin, jnp.zeros((seq, pad), F32)], axis=-1)
    return cos_t, sin_t


def _qkv_proj(h, seq, w_qkv):
    t, d = h.shape
    tm = _pick_tile(seq, 1024, MOBA_BLOCK)
    assert tm // MOBA_BLOCK <= SUBLANES
    tiles_per_seq = seq // tm
    cos_t, sin_t = _rope_tables(seq)
    qkv, ksum = pl.pallas_call(
        _qkv_kernel,
        grid=(t // tm, 3),
        in_specs=[
            pl.BlockSpec((tm, d), lambda i, j: (i, 0)),
            pl.BlockSpec((d, d), lambda i, j: (0, _qkv_pass_col(j))),
            pl.BlockSpec((tm, HEAD_DIM), lambda i, j: (i % tiles_per_seq, 0)),
            pl.BlockSpec((tm, HEAD_DIM), lambda i, j: (i % tiles_per_seq, 0)),
        ],
        out_specs=[
            pl.BlockSpec((tm, d), lambda i, j: (i, _qkv_pass_col(j))),
            pl.BlockSpec((1, SUBLANES, d), lambda i, j: (i, 0, 0)),
        ],
        out_shape=[
            jax.ShapeDtypeStruct((t, 3 * d), BF16),
            jax.ShapeDtypeStruct((t // tm, SUBLANES, d), F32),
        ],
        compiler_params=_params(("arbitrary", "arbitrary")),
        name="qkv_rope_proj",
    )(h, w_qkv, cos_t, sin_t)
    ksum = ksum[:, :tm // MOBA_BLOCK, :].reshape(t // MOBA_BLOCK, d)
    return qkv, ksum


MOBA_HEADS_PER_STEP = 2


def _moba_kernel(ins, outs, scratch, cast_pieces):
    q_ref, k_ref, v_ref, ksum_ref = ins
    (o_ref,), (s_ref,) = outs, scratch
    cast_pieces()
    heads = []
    for hh in range(q_ref.shape[1] // HEAD_DIM):
        cols = pl.ds(hh * HEAD_DIM, HEAD_DIM)
        heads.append((q_ref.at[:, cols], k_ref.at[:, cols], v_ref.at[:, cols], ksum_ref.at[:, cols],
                      o_ref.at[:, cols], s_ref.at[hh]))
    maxes = [_moba_scores(q, k, ksum, s) for q, k, v, ksum, o, s in heads]
    for (q, k, v, ksum, o, s), m_of in zip(heads, maxes):
        _moba_outputs(v, o, s, m_of)


def _tile_index(i, jb):
    return i * (i + 1) // 2 + jb


def _moba_scores(q_ref, k_ref, ksum_ref, s_ref):
    seq = q_ref.shape[0]
    nb = seq // MOBA_BLOCK
    blk = MOBA_BLOCK
    nt = (((1,), (1,)), ((), ()))

    kmean = (ksum_ref[...] * (1.0 / blk)).astype(BF16)
    gate = lax.dot_general(kmean, q_ref[...], nt, preferred_element_type=F32)
    n_idx = lax.broadcasted_iota(jnp.int32, (nb, seq), 0)
    q_blk = lax.broadcasted_iota(jnp.int32, (nb, seq), 1) // blk
    rank = jnp.zeros((nb, seq), jnp.int32)
    for m in range(nb - 1):
        gm = gate[m:m + 1, :]
        beats = (gm > gate) | ((gm == gate) & (m < n_idx))
        rank = rank + jnp.where(beats & (m < q_blk), 1, 0)
    sel = jnp.where((n_idx < q_blk) & (rank < MOBA_TOPK), 1.0, 0.0).astype(F32)

    key_i = lax.broadcasted_iota(jnp.int32, (blk, blk), 0)
    qry_i = lax.broadcasted_iota(jnp.int32, (blk, blk), 1)
    causal = key_i <= qry_i

    m_of = {}
    for i in reversed(range(nb)):
        qi = q_ref[i * blk:(i + 1) * blk, :]
        m_run = None
        for jb in range(i + 1):
            kj = k_ref[jb * blk:(jb + 1) * blk, :]
            s = lax.dot_general(kj, qi, nt, preferred_element_type=F32)
            if jb == i:
                s = jnp.where(causal, s, NEG)
            elif i > MOBA_TOPK:
                s = jnp.where(sel[jb:jb + 1, i * blk:(i + 1) * blk] > 0.5, s, NEG)
            s_ref[_tile_index(i, jb)] = s
            m_tile = jnp.max(s.reshape(blk // SUBLANES, SUBLANES, blk), axis=0)
            m_run = m_tile if m_run is None else jnp.maximum(m_run, m_tile)
        m_of[i] = jnp.max(m_run, axis=0, keepdims=True)
    return m_of


def _moba_outputs(v_ref, o_ref, s_ref, m_of):
    blk = MOBA_BLOCK
    tn = (((0,), (0,)), ((), ()))
    exp2_scale = HEAD_DIM ** -0.5 * LOG2_E
    for i in sorted(m_of, reverse=True):
        l_run = None
        acc = None
        for jb in range(i + 1):
            p = jnp.exp2((s_ref[_tile_index(i, jb)] - m_of[i]) * exp2_scale)
            l_tile = jnp.sum(p.reshape(blk // SUBLANES, SUBLANES, blk), axis=0)
            l_run = l_tile if l_run is None else l_run + l_tile
            vj = v_ref[jb * blk:(jb + 1) * blk, :]
            pv = lax.dot_general(vj, p.astype(BF16), tn, preferred_element_type=F32)
            acc = pv if acc is None else acc + pv
        l = jnp.sum(l_run, axis=0, keepdims=True)
        o_ref[i * blk:(i + 1) * blk, :] = (acc / l).T.astype(BF16)


def _moba_attention(qkv, ksum, bsz, seq, cast_next):
    t, d3 = qkv.shape
    d = d3 // 3
    hp = MOBA_HEADS_PER_STEP if (d // HEAD_DIM) % MOBA_HEADS_PER_STEP == 0 else 1
    nh = d // (HEAD_DIM * hp)
    nb = seq // MOBA_BLOCK
    width = hp * HEAD_DIM
    cast_arrays, cast_in, cast_out, cast_shapes = _cast_jobs(
        cast_next, bsz * nh, lambda b, h: b * nh + h)
    out, *cast = pl.pallas_call(
        functools.partial(_with_cast_jobs, _moba_kernel, 4, 1, len(cast_arrays)),
        grid=(bsz, nh),
        in_specs=[
            pl.BlockSpec((seq, width), lambda b, h: (b, h)),
            pl.BlockSpec((seq, width), lambda b, h: (b, nh + h)),
            pl.BlockSpec((seq, width), lambda b, h: (b, 2 * nh + h)),
            pl.BlockSpec((nb, width), lambda b, h: (b, h)),
        ] + cast_in,
        out_specs=[pl.BlockSpec((seq, width), lambda b, h: (b, h))] + cast_out,
        out_shape=[jax.ShapeDtypeStruct((t, d), BF16)] + cast_shapes,
        scratch_shapes=[pltpu.VMEM((hp, nb * (nb + 1) // 2, MOBA_BLOCK, MOBA_BLOCK), F32)],
        compiler_params=_params(("arbitrary", "arbitrary")),
        name="moba_attention",
    )(qkv, qkv, qkv, ksum, *cast_arrays)
    return out, cast


def _oproj_kernel(x_ref, o_ref, w_ref, gpost_ref, out_ref):
    m = jnp.dot(o_ref[...], w_ref[...], preferred_element_type=F32)
    out_ref[...] = x_ref[...] + _rms(m, gpost_ref[...])


def _out_proj(x, o, w_o, g_post):
    t, d = x.shape
    tm = _pick_tile(t, 512, SUBLANES)
    return pl.pallas_call(
        _oproj_kernel,
        grid=(t // tm,),
        in_specs=[
            pl.BlockSpec((tm, d), lambda i: (i, 0)),
            pl.BlockSpec((tm, d), lambda i: (i, 0)),
            pl.BlockSpec((d, d), lambda i: (0, 0)),
            pl.BlockSpec((1, d), lambda i: (0, 0)),
        ],
        out_specs=pl.BlockSpec((tm, d), lambda i: (i, 0)),
        out_shape=jax.ShapeDtypeStruct((t, d), F32),
        compiler_params=_params(("arbitrary",)),
        name="attn_out_proj",
    )(x, o, w_o, g_post)


def kernel(x, norm_g, ffn_w_in, ffn_w_out, conv_w_in, conv_w, conv_w_out, attn_w_qkv, attn_w_o):
    bsz, seq, d = x.shape
    depth = norm_g.shape[0]
    assert d % HEAD_DIM == 0 and seq % MOBA_BLOCK == 0
    assert min(MOBA_TOPK, seq // MOBA_BLOCK - 1) == MOBA_TOPK
    h = x.reshape(bsz * seq, d)

    def ffn_weights(i, half):
        return [(ffn_w_in, (i, half)), (ffn_w_out, (i, half))]

    def mixer_weights(i):
        if i % 2 == 0:
            return [(conv_w_in, (i // 2,)), (conv_w_out, (i // 2,))]
        return [(attn_w_qkv, (i // 2,)), (attn_w_o, (i // 2,))]

    w_in, w_out = ffn_w_in[0, 0].astype(BF16), ffn_w_out[0, 0].astype(BF16)
    for i in range(depth):
        g = norm_g[i][:, None, :]
        if i % 2 == 0:
            h, _, (wm_a, wm_b) = _ffn(h, g[0], g[1], w_in, w_out, mixer_weights(i))
            h, _, (w_in, w_out) = _conv_mixer(h, seq, g[2], g[3], wm_a, conv_w[i // 2], wm_b,
                                              ffn_weights(i, 1))
        else:
            h, hn, (wm_a, wm_b) = _ffn(h, g[0], g[1], w_in, w_out, mixer_weights(i), g_next=g[2])
            qkv, ksum = _qkv_proj(hn, seq, wm_a)
            o, (w_in, w_out) = _moba_attention(qkv, ksum, bsz, seq, ffn_weights(i, 1))
            h = _out_proj(h, o, wm_b, g[3])
        last = i == depth - 1
        h, _, nxt = _ffn(h, g[4], g[5], w_in, w_out, [] if last else ffn_weights(i + 1, 0))
        if not last:
            w_in, w_out = nxt
    return h.reshape(bsz, seq, d)
```

```python
import functools

import jax
import jax.numpy as jnp
from jax import lax
from jax.experimental import pallas as pl
from jax.experimental.pallas import tpu as pltpu

HEAD_DIM = 128
ROT_DIM = HEAD_DIM // 4
ROPE_THETA = 500000.0
MOBA_BLOCK = 256
MOBA_TOPK = 3
CONV_WIDTH = 3
RMS_EPS = 1e-6
NEG = -1e30
LOG2_E = 1.4426950408889634

LANES = 128
SUBLANES = 8
MXU_WIDTH = 256
V7X_VMEM_LIMIT_BYTES = 56 * 1024 * 1024

F32 = jnp.float32
BF16 = jnp.bfloat16


def _pick_tile(n, target, multiple):
    t = min(n, target)
    t -= t % multiple
    while t > multiple and n % t:
        t -= multiple
    assert t > 0 and n % t == 0, (n, target, multiple)
    return t


def _rms(x, g):
    ms = jnp.mean(x * x, axis=-1, keepdims=True)
    return (x * lax.rsqrt(ms + RMS_EPS)) * g


def _params(sem):
    return pltpu.CompilerParams(dimension_semantics=sem,
                                vmem_limit_bytes=V7X_VMEM_LIMIT_BYTES)


def _stacked_spec(lead, block, index_fn):
    lead = tuple(lead)
    return pl.BlockSpec((None,) * len(lead) + tuple(block), lambda *g: lead + tuple(index_fn(*g)))


BF16_SUBLANES = 16


def _slices_per_tile(tm, nj):
    n = min(nj, tm // BF16_SUBLANES)
    while tm % n or (tm // n) % BF16_SUBLANES:
        n -= 1
    return n


def _dependent_zero(*vals):
    words = []
    for v in vals:
        bits = pltpu.bitcast(v, jnp.uint32)
        words += [bits[s:s + SUBLANES, c:c + LANES]
                  for c in range(0, bits.shape[1], LANES) for s in range(0, bits.shape[0], SUBLANES)]
    while len(words) > 1:
        words = [a | b for a, b in zip(words[0::2], words[1::2])] + words[len(words) & ~1:]
    return (words[0] >> 16) >> 16


def _three_stage_step(n_tiles, n_slices, res_scale, down_lhs, stage_refs, wo_ref, outs, h_ref,
                      acc_ref, cast_pieces):
    xn_ref, xf_ref, gpre_ref, gpost_ref = stage_refs[:4]
    o_ref = outs[0]
    r = pl.program_id(0)
    j = pl.program_id(1)
    sr = xn_ref.shape[0]
    rows = pl.ds(pl.multiple_of(jnp.minimum(j, n_slices - 1) * sr, sr), sr)
    norm_slot = r % 2
    mm_slot = 1 - norm_slot
    cast_pieces()

    @pl.when((r == 0) & (j == 0))
    def _():
        acc_ref[...] = jnp.zeros_like(acc_ref)

    def finalize_slice():
        out = xf_ref[...] + res_scale * _rms(acc_ref[norm_slot, rows, :], gpost_ref[...])
        o_ref[...] = out
        if len(outs) == 1:
            return [out]
        nxt = _rms(out, stage_refs[4][...])
        outs[1][...] = nxt.astype(BF16)
        return [out, nxt]

    def norm_slice():
        hn = _rms(xn_ref[...], gpre_ref[...])
        h_ref[norm_slot, rows, :] = hn.astype(BF16)
        return hn

    def anchor(v):
        zero = _dependent_zero(*finalize_slice(), norm_slice())
        zero = jnp.tile(zero, (v.shape[0] // SUBLANES, v.shape[1] // LANES))
        return pltpu.bitcast(pltpu.bitcast(v, jnp.uint32) | zero, F32)

    has_matmul = (r >= 1) & (r <= n_tiles)

    @pl.when(has_matmul)
    def _():
        k0 = 0
        for a in down_lhs(h_ref[mm_slot], anchor, r - 1):
            kc = a.shape[1]
            prev = jnp.where(j == 0, 0.0, acc_ref[mm_slot]) if k0 == 0 else acc_ref[mm_slot]
            acc_ref[mm_slot] = jnp.dot(a, wo_ref[k0:k0 + kc, :], preferred_element_type=F32) + prev
            k0 += kc

    @pl.when(jnp.logical_not(has_matmul))
    def _():
        finalize_slice()
        norm_slice()


def _cast_piece_rows(rows, steps):
    for pr in range(BF16_SUBLANES, rows + 1, BF16_SUBLANES):
        if rows % pr == 0 and rows // pr <= steps:
            return pr
    raise ValueError((rows, steps))


def _cast_jobs(cast_next, steps, step_index):
    arrays, in_specs, out_specs, out_shapes = [], [], [], []
    for w, lead in cast_next:
        rows, cols = w.shape[-2:]
        pr = _cast_piece_rows(rows, steps)

        def piece(*g, last=rows // pr - 1):
            return (jnp.minimum(step_index(*g), last), 0)

        arrays.append(w)
        in_specs.append(_stacked_spec(lead, (pr, cols), piece))
        out_specs.append(pl.BlockSpec((pr, cols), piece))
        out_shapes.append(jax.ShapeDtypeStruct((rows, cols), BF16))
    return arrays, in_specs, out_specs, out_shapes


def _with_cast_jobs(body, n_in, n_out, n_cast, *refs):
    o0 = n_in + n_cast

    def cast_pieces():
        vals = [src[...] for src in refs[n_in:o0]]
        for v, dst in zip(vals, refs[o0 + n_out:o0 + n_out + n_cast]):
            dst[...] = v.astype(BF16)
        return vals

    body(refs[:n_in], refs[o0:o0 + n_out], refs[o0 + n_out + n_cast:], cast_pieces)


def _three_stage_call(name, body, x, g_pre, g_post, g_next, tm, nj, chunked_inputs,
                      scratch_shapes, cast_next):
    t, d = x.shape
    n_tiles = t // tm
    n_slices = _slices_per_tile(tm, nj)
    sr = tm // n_slices
    cast_arrays, cast_in, cast_out, cast_shapes = _cast_jobs(
        cast_next, (n_tiles + 2) * nj, lambda r, j: r * nj + j)
    gains = [g_pre, g_post] + ([] if g_next is None else [g_next])
    n_out = 1 if g_next is None else 2

    def norm_slice(r, j):
        return (jnp.minimum(r, n_tiles - 1) * n_slices + jnp.minimum(j, n_slices - 1), 0)

    def final_slice(r, j):
        return (jnp.where(r < 2, 0, (r - 2) * n_slices + jnp.minimum(j, n_slices - 1)), 0)

    def chunk(r, j):
        return jnp.where((r >= 1) & (r <= n_tiles), j, 0)

    results = pl.pallas_call(
        functools.partial(_with_cast_jobs,
                          functools.partial(body, n_tiles, n_slices, 2 + len(gains)),
                          2 + len(gains) + len(chunked_inputs), n_out, len(cast_arrays)),
        grid=(n_tiles + 2, nj),
        in_specs=[pl.BlockSpec((sr, d), norm_slice), pl.BlockSpec((sr, d), final_slice)]
        + [pl.BlockSpec((1, d), lambda r, j: (0, 0)) for _ in gains]
        + [spec_fn(chunk) for _, spec_fn in chunked_inputs] + cast_in,
        out_specs=[pl.BlockSpec((sr, d), final_slice)] * n_out + cast_out,
        out_shape=[jax.ShapeDtypeStruct((t, d), F32)]
        + [jax.ShapeDtypeStruct((t, d), BF16)] * (n_out - 1) + cast_shapes,
        scratch_shapes=[pltpu.VMEM((2, tm, d), BF16), pltpu.VMEM((2, tm, d), F32)] + scratch_shapes,
        compiler_params=_params(("arbitrary", "arbitrary")),
        name=name,
    )(x, x, *gains, *[a for a, _ in chunked_inputs], *cast_arrays)
    return results[0], (results[1] if n_out == 2 else None), results[n_out:]


def _ffn_kernel(n_tiles, n_slices, n_stage, ins, outs, scratch, cast_pieces):
    wg_ref, wu_ref, wo_ref = ins[n_stage:]
    h_ref, acc_ref = scratch

    def down_lhs(h, anchor, tile):
        groups = []
        for c in range(0, wg_ref.shape[1], MXU_WIDTH):
            g = jnp.dot(h, wg_ref[:, c:c + MXU_WIDTH], preferred_element_type=F32)
            u = jnp.dot(h, wu_ref[:, c:c + MXU_WIDTH], preferred_element_type=F32)
            if c == 0:
                u = anchor(u)
            groups.append((jax.nn.silu(g) * u).astype(BF16))
        return groups

    _three_stage_step(n_tiles, n_slices, 0.5, down_lhs, ins[:n_stage], wo_ref, outs, h_ref,
                      acc_ref, cast_pieces)


def _ffn(x, g_pre, g_post, w_in, w_out, cast_next, g_next=None):
    t, d = x.shape
    f = w_out.shape[-2]
    tm = _pick_tile(t, 1024, BF16_SUBLANES)
    tf = _pick_tile(f, 512, LANES)
    nj = f // tf
    return _three_stage_call(
        "ffn_half_step", _ffn_kernel, x, g_pre, g_post, g_next, tm, nj,
        [(w_in, lambda chunk: pl.BlockSpec((d, tf), lambda r, j: (0, chunk(r, j)))),
         (w_in, lambda chunk: pl.BlockSpec((d, tf), lambda r, j: (0, nj + chunk(r, j)))),
         (w_out, lambda chunk: pl.BlockSpec((tf, d), lambda r, j: (chunk(r, j), 0)))],
        [], cast_next)


def _conv_kernel(tiles_per_seq, n_tiles, n_slices, n_stage, ins, outs, scratch, cast_pieces):
    wb_ref, wc_ref, wu_ref, cw_ref, wo_ref = ins[n_stage:]
    h_ref, acc_ref, zbuf_ref, halo_ref = scratch
    j = pl.program_id(1)
    tm = h_ref.shape[1]

    @pl.when((pl.program_id(0) == 0) & (j == 0))
    def _():
        halo_ref[...] = jnp.zeros_like(halo_ref)

    def down_lhs(h, anchor, tile):
        groups = []
        for c0 in range(0, wb_ref.shape[1], MXU_WIDTH):
            cols = slice(c0, c0 + MXU_WIDTH)
            b = jnp.dot(h, wb_ref[:, cols], preferred_element_type=F32)
            c = jnp.dot(h, wc_ref[:, cols], preferred_element_type=F32)
            u = jnp.dot(h, wu_ref[:, cols], preferred_element_type=F32)
            if c0 == 0:
                u = anchor(u)
            z = c * u

            zbuf_ref[pl.ds(SUBLANES, tm), cols] = z
            prev = halo_ref[j, :, cols]
            zbuf_ref[pl.ds(0, SUBLANES), cols] = jnp.where(tile % tiles_per_seq == 0,
                                                           jnp.zeros_like(prev), prev)
            halo_ref[j, :, cols] = z[tm - SUBLANES:, :]

            cw = cw_ref[:, cols]
            zc = (cw[0:1, :] * zbuf_ref[pl.ds(SUBLANES - 2, tm), cols]
                  + cw[1:2, :] * zbuf_ref[pl.ds(SUBLANES - 1, tm), cols]
                  + cw[2:3, :] * z)
            groups.append((b * zc).astype(BF16))
        return groups

    _three_stage_step(n_tiles, n_slices, 1.0, down_lhs, ins[:n_stage], wo_ref, outs, h_ref,
                      acc_ref, cast_pieces)


def _conv_mixer(x, seq, g_pre, g_post, w_in, conv_w, w_out, cast_next):
    t, d = x.shape
    tm = _pick_tile(seq, 512, BF16_SUBLANES)
    tn = _pick_tile(d, 512, LANES)
    nj = d // tn

    def w_in_cols(part):
        return lambda chunk: pl.BlockSpec((d, tn), lambda r, j: (0, part * nj + chunk(r, j)))

    return _three_stage_call(
        "short_conv_mixer", functools.partial(_conv_kernel, seq // tm), x, g_pre, g_post, None,
        tm, nj,
        [(w_in, w_in_cols(0)), (w_in, w_in_cols(1)), (w_in, w_in_cols(2)),
         (conv_w, lambda chunk: pl.BlockSpec((CONV_WIDTH, tn), lambda r, j: (0, chunk(r, j)))),
         (w_out, lambda chunk: pl.BlockSpec((tn, d), lambda r, j: (chunk(r, j), 0)))],
        [pltpu.VMEM((tm + SUBLANES, tn), F32), pltpu.VMEM((nj, SUBLANES, tn), F32)], cast_next)


def _qkv_pass_col(j):
    return (2 * j) % 3


def _qkv_kernel(h_ref, w_ref, cos_ref, sin_ref, o_ref, ksum_ref):
    j = pl.program_id(1)
    tm, d = h_ref.shape
    half = ROT_DIM // 2
    nb = tm // MOBA_BLOCK

    rotary = j != 1
    cos = jnp.where(rotary, cos_ref[...], 1.0)
    sin = jnp.where(rotary, sin_ref[...], 0.0)
    lane = lax.broadcasted_iota(jnp.int32, (tm, HEAD_DIM), 1)
    pad_rows = jnp.zeros((SUBLANES - nb, HEAD_DIM), F32)
    h = h_ref[...]
    for c in range(d // MXU_WIDTH):
        r = jnp.dot(h, w_ref[:, c * MXU_WIDTH:(c + 1) * MXU_WIDTH], preferred_element_type=F32)
        for hh in range(MXU_WIDTH // HEAD_DIM):
            th = r[:, hh * HEAD_DIM:(hh + 1) * HEAD_DIM]
            partner = jnp.where(lane < half,
                                pltpu.roll(th, HEAD_DIM - half, 1),
                                pltpu.roll(th, half, 1))
            rot = th * cos + partner * sin
            cols = slice(c * MXU_WIDTH + hh * HEAD_DIM, c * MXU_WIDTH + (hh + 1) * HEAD_DIM)
            o_ref[:, cols] = rot.astype(BF16)
            sums = jnp.sum(rot.reshape(nb, MOBA_BLOCK, HEAD_DIM), axis=1)
            ksum_ref[0, :, cols] = jnp.concatenate([sums, pad_rows], axis=0) if nb < SUBLANES else sums


def _rope_tables(seq):
    half = ROT_DIM // 2
    inv_freq = ROPE_THETA ** (-jnp.arange(0, ROT_DIM, 2, dtype=F32) / ROT_DIM)
    ang = jnp.arange(seq, dtype=F32)[:, None] * inv_freq[None, :]
    cos, sin = jnp.cos(ang), jnp.sin(ang)
    pad = HEAD_DIM - ROT_DIM
    cos_t = jnp.concatenate([cos, cos, jnp.ones((seq, pad), F32)], axis=-1)
    sin_t = jnp.concatenate([-sin, sin, jnp.zeros((seq, pad), F32)], axis=-1)
    return cos_t, sin_t


def _qkv_proj(h, seq, w_qkv):
    t, d = h.shape
    tm = _pick_tile(seq, 1024, MOBA_BLOCK)
    assert tm // MOBA_BLOCK <= SUBLANES
    tiles_per_seq = seq // tm
    cos_t, sin_t = _rope_tables(seq)
    qkv, ksum = pl.pallas_call(
        _qkv_kernel,
        grid=(t // tm, 3),
        in_specs=[
            pl.BlockSpec((tm, d), lambda i, j: (i, 0)),
            pl.BlockSpec((d, d), lambda i, j: (0, _qkv_pass_col(j))),
            pl.BlockSpec((tm, HEAD_DIM), lambda i, j: (i % tiles_per_seq, 0)),
            pl.BlockSpec((tm, HEAD_DIM), lambda i, j: (i % tiles_per_seq, 0)),
        ],
        out_specs=[
            pl.BlockSpec((tm, d), lambda i, j: (i, _qkv_pass_col(j))),
            pl.BlockSpec((1, SUBLANES, d), lambda i, j: (i, 0, 0)),
        ],
        out_shape=[
            jax.ShapeDtypeStruct((t, 3 * d), BF16),
            jax.ShapeDtypeStruct((t // tm, SUBLANES, d), F32),
        ],
        compiler_params=_params(("arbitrary", "arbitrary")),
        name="qkv_rope_proj",
    )(h, w_qkv, cos_t, sin_t)
    ksum = ksum[:, :tm // MOBA_BLOCK, :].reshape(t // MOBA_BLOCK, d)
    return qkv, ksum


MOBA_HEADS_PER_STEP = 2


def _moba_kernel(ins, outs, scratch, cast_pieces):
    q_ref, k_ref, v_ref, ksum_ref = ins
    (o_ref,), (s_ref,) = outs, scratch
    cast_pieces()
    heads = []
    for hh in range(q_ref.shape[1] // HEAD_DIM):
        cols = pl.ds(hh * HEAD_DIM, HEAD_DIM)
        heads.append((q_ref.at[:, cols], k_ref.at[:, cols], v_ref.at[:, cols], ksum_ref.at[:, cols],
                      o_ref.at[:, cols], s_ref.at[hh]))
    maxes = [_moba_scores(q, k, ksum, s) for q, k, v, ksum, o, s in heads]
    for (q, k, v, ksum, o, s), m_of in zip(heads, maxes):
        _moba_outputs(v, o, s, m_of)


def _tile_index(i, jb):
    return i * (i + 1) // 2 + jb


def _moba_scores(q_ref, k_ref, ksum_ref, s_ref):
    seq = q_ref.shape[0]
    nb = seq // MOBA_BLOCK
    blk = MOBA_BLOCK
    nt = (((1,), (1,)), ((), ()))

    kmean = (ksum_ref[...] * (1.0 / blk)).astype(BF16)
    gate = lax.dot_general(kmean, q_ref[...], nt, preferred_element_type=F32)
    n_idx = lax.broadcasted_iota(jnp.int32, (nb, seq), 0)
    q_blk = lax.broadcasted_iota(jnp.int32, (nb, seq), 1) // blk
    rank = jnp.zeros((nb, seq), jnp.int32)
    for m in range(nb - 1):
        gm = gate[m:m + 1, :]
        beats = (gm > gate) | ((gm == gate) & (m < n_idx))
        rank = rank + jnp.where(beats & (m < q_blk), 1, 0)
    sel = jnp.where((n_idx < q_blk) & (rank < MOBA_TOPK), 1.0, 0.0).astype(F32)

    key_i = lax.broadcasted_iota(jnp.int32, (blk, blk), 0)
    qry_i = lax.broadcasted_iota(jnp.int32, (blk, blk), 1)
    causal = key_i <= qry_i

    m_of = {}
    for i in reversed(range(nb)):
        qi = q_ref[i * blk:(i + 1) * blk, :]
        m_run = None
        for jb in range(i + 1):
            kj = k_ref[jb * blk:(jb + 1) * blk, :]
            s = lax.dot_general(kj, qi, nt, preferred_element_type=F32)
            if jb == i:
                s = jnp.where(causal, s, NEG)
            elif i > MOBA_TOPK:
                s = jnp.where(sel[jb:jb + 1, i * blk:(i + 1) * blk] > 0.5, s, NEG)
            s_ref[_tile_index(i, jb)] = s
            m_tile = jnp.max(s.reshape(blk // SUBLANES, SUBLANES, blk), axis=0)
            m_run = m_tile if m_run is None else jnp.maximum(m_run, m_tile)
        m_of[i] = jnp.max(m_run, axis=0, keepdims=True)
    return m_of


def _moba_outputs(v_ref, o_ref, s_ref, m_of):
    blk = MOBA_BLOCK
    tn = (((0,), (0,)), ((), ()))
    exp2_scale = HEAD_DIM ** -0.5 * LOG2_E
    for i in sorted(m_of, reverse=True):
        l_run = None
        acc = None
        for jb in range(i + 1):
            p = jnp.exp2((s_ref[_tile_index(i, jb)] - m_of[i]) * exp2_scale)
            l_tile = jnp.sum(p.reshape(blk // SUBLANES, SUBLANES, blk), axis=0)
            l_run = l_tile if l_run is None else l_run + l_tile
            vj = v_ref[jb * blk:(jb + 1) * blk, :]
            pv = lax.dot_general(vj, p.astype(BF16), tn, preferred_element_type=F32)
            acc = pv if acc is None else acc + pv
        l = jnp.sum(l_run, axis=0, keepdims=True)
        o_ref[i * blk:(i + 1) * blk, :] = (acc / l).T.astype(BF16)


def _moba_attention(qkv, ksum, bsz, seq, cast_next):
    t, d3 = qkv.shape
    d = d3 // 3
    hp = MOBA_HEADS_PER_STEP if (d // HEAD_DIM) % MOBA_HEADS_PER_STEP == 0 else 1
    nh = d // (HEAD_DIM * hp)
    nb = seq // MOBA_BLOCK
    width = hp * HEAD_DIM
    cast_arrays, cast_in, cast_out, cast_shapes = _cast_jobs(
        cast_next, bsz * nh, lambda b, h: b * nh + h)
    out, *cast = pl.pallas_call(
        functools.partial(_with_cast_jobs, _moba_kernel, 4, 1, len(cast_arrays)),
        grid=(bsz, nh),
        in_specs=[
            pl.BlockSpec((seq, width), lambda b, h: (b, h)),
            pl.BlockSpec((seq, width), lambda b, h: (b, nh + h)),
            pl.BlockSpec((seq, width), lambda b, h: (b, 2 * nh + h)),
            pl.BlockSpec((nb, width), lambda b, h: (b, h)),
        ] + cast_in,
        out_specs=[pl.BlockSpec((seq, width), lambda b, h: (b, h))] + cast_out,
        out_shape=[jax.ShapeDtypeStruct((t, d), BF16)] + cast_shapes,
        scratch_shapes=[pltpu.VMEM((hp, nb * (nb + 1) // 2, MOBA_BLOCK, MOBA_BLOCK), F32)],
        compiler_params=_params(("arbitrary", "arbitrary")),
        name="moba_attention",
    )(qkv, qkv, qkv, ksum, *cast_arrays)
    return out, cast


def _oproj_kernel(x_ref, o_ref, w_ref, gpost_ref, out_ref):
    m = jnp.dot(o_ref[...], w_ref[...], preferred_element_type=F32)
    out_ref[...] = x_ref[...] + _rms(m, gpost_ref[...])


def _out_proj(x, o, w_o, g_post):
    t, d = x.shape
    tm = _pick_tile(t, 512, SUBLANES)
    return pl.pallas_call(
        _oproj_kernel,
        grid=(t // tm,),
        in_specs=[
            pl.BlockSpec((tm, d), lambda i: (i, 0)),
            pl.BlockSpec((tm, d), lambda i: (i, 0)),
            pl.BlockSpec((d, d), lambda i: (0, 0)),
            pl.BlockSpec((1, d), lambda i: (0, 0)),
        ],
        out_specs=pl.BlockSpec((tm, d), lambda i: (i, 0)),
        out_shape=jax.ShapeDtypeStruct((t, d), F32),
        compiler_params=_params(("arbitrary",)),
        name="attn_out_proj",
    )(x, o, w_o, g_post)


def kernel(x, norm_g, ffn_w_in, ffn_w_out, conv_w_in, conv_w, conv_w_out, attn_w_qkv, attn_w_o):
    bsz, seq, d = x.shape
    depth = norm_g.shape[0]
    assert d % HEAD_DIM == 0 and seq % MOBA_BLOCK == 0
    assert min(MOBA_TOPK, seq // MOBA_BLOCK - 1) == MOBA_TOPK
    h = x.reshape(bsz * seq, d)

    def ffn_weights(i, half):
        return [(ffn_w_in, (i, half)), (ffn_w_out, (i, half))]

    def mixer_weights(i):
        if i % 2 == 0:
            return [(conv_w_in, (i // 2,)), (conv_w_out, (i // 2,))]
        return [(attn_w_qkv, (i // 2,)), (attn_w_o, (i // 2,))]

    w_in, w_out = ffn_w_in[0, 0].astype(BF16), ffn_w_out[0, 0].astype(BF16)
    for i in range(depth):
        g = norm_g[i][:, None, :]
        if i % 2 == 0:
            h, _, (wm_a, wm_b) = _ffn(h, g[0], g[1], w_in, w_out, mixer_weights(i))
            h, _, (w_in, w_out) = _conv_mixer(h, seq, g[2], g[3], wm_a, conv_w[i // 2], wm_b,
                                              ffn_weights(i, 1))
        else:
            h, hn, (wm_a, wm_b) = _ffn(h, g[0], g[1], w_in, w_out, mixer_weights(i), g_next=g[2])
            qkv, ksum = _qkv_proj(hn, seq, wm_a)
            o, (w_in, w_out) = _moba_attention(qkv, ksum, bsz, seq, ffn_weights(i, 1))
            h = _out_proj(h, o, wm_b, g[3])
        last = i == depth - 1
        h, _, nxt = _ffn(h, g[4], g[5], w_in, w_out, [] if last else ffn_weights(i + 1, 0))
        if not last:
            w_in, w_out = nxt
    return h.reshape(bsz, seq, d)
```

```python
import functools

import jax
import jax.numpy as jnp
from jax import lax
from jax.experimental import pallas as pl
from jax.experimental.pallas import tpu as pltpu

HEAD_DIM = 128
ROT_DIM = HEAD_DIM // 4
ROPE_THETA = 500000.0
MOBA_BLOCK = 256
MOBA_TOPK = 3
CONV_WIDTH = 3
RMS_EPS = 1e-6
NEG = -1e30
LOG2_E = 1.4426950408889634

LANES = 128
SUBLANES = 8
MXU_WIDTH = 256
V7X_VMEM_LIMIT_BYTES = 56 * 1024 * 1024

F32 = jnp.float32
BF16 = jnp.bfloat16


def _pick_tile(n, target, multiple):
    t = min(n, target)
    t -= t % multiple
    while t > multiple and n % t:
        t -= multiple
    assert t > 0 and n % t == 0, (n, target, multiple)
    return t


def _rms(x, g):
    ms = jnp.mean(x * x, axis=-1, keepdims=True)
    return (x * lax.rsqrt(ms + RMS_EPS)) * g


def _params(sem):
    return pltpu.CompilerParams(dimension_semantics=sem,
                                vmem_limit_bytes=V7X_VMEM_LIMIT_BYTES)


def _stacked_spec(lead, block, index_fn):
    lead = tuple(lead)
    return pl.BlockSpec((None,) * len(lead) + tuple(block), lambda *g: lead + tuple(index_fn(*g)))


BF16_SUBLANES = 16


def _slices_per_tile(tm, nj):
    n = min(nj, tm // BF16_SUBLANES)
    while tm % n or (tm // n) % BF16_SUBLANES:
        n -= 1
    return n


def _dependent_zero(*vals):
    words = []
    for v in vals:
        bits = pltpu.bitcast(v, jnp.uint32)
        words += [bits[s:s + SUBLANES, c:c + LANES]
                  for c in range(0, bits.shape[1], LANES) for s in range(0, bits.shape[0], SUBLANES)]
    while len(words) > 1:
        words = [a | b for a, b in zip(words[0::2], words[1::2])] + words[len(words) & ~1:]
    return (words[0] >> 16) >> 16


def _three_stage_step(n_tiles, n_slices, res_scale, down_lhs, xn_ref, xf_ref, gpre_ref, gpost_ref,
                      wo_ref, o_ref, h_ref, acc_ref):
    r = pl.program_id(0)
    j = pl.program_id(1)
    sr = xn_ref.shape[0]
    rows = pl.ds(pl.multiple_of(jnp.minimum(j, n_slices - 1) * sr, sr), sr)
    norm_slot = r % 2
    mm_slot = 1 - norm_slot

    @pl.when((r == 0) & (j == 0))
    def _():
        acc_ref[...] = jnp.zeros_like(acc_ref)

    def finalize_slice():
        out = xf_ref[...] + res_scale * _rms(acc_ref[norm_slot, rows, :], gpost_ref[...])
        o_ref[...] = out
        return out

    def norm_slice():
        hn = _rms(xn_ref[...], gpre_ref[...])
        h_ref[norm_slot, rows, :] = hn.astype(BF16)
        return hn

    def anchor(v):
        zero = _dependent_zero(finalize_slice(), norm_slice())
        zero = jnp.tile(zero, (v.shape[0] // SUBLANES, v.shape[1] // LANES))
        return pltpu.bitcast(pltpu.bitcast(v, jnp.uint32) | zero, F32)

    has_matmul = (r >= 1) & (r <= n_tiles)

    @pl.when(has_matmul)
    def _():
        k0 = 0
        for a in down_lhs(h_ref[mm_slot], anchor, r - 1):
            kc = a.shape[1]
            prev = jnp.where(j == 0, 0.0, acc_ref[mm_slot]) if k0 == 0 else acc_ref[mm_slot]
            acc_ref[mm_slot] = jnp.dot(a, wo_ref[k0:k0 + kc, :], preferred_element_type=F32) + prev
            k0 += kc

    @pl.when(jnp.logical_not(has_matmul))
    def _():
        finalize_slice()
        norm_slice()


def _cast_piece_rows(rows, steps):
    for pr in range(BF16_SUBLANES, rows + 1, BF16_SUBLANES):
        if rows % pr == 0 and rows // pr <= steps:
            return pr
    raise ValueError((rows, steps))


def _cast_jobs(cast_next, steps, step_index):
    arrays, in_specs, out_specs, out_shapes = [], [], [], []
    for w, lead in cast_next:
        rows, cols = w.shape[-2:]
        pr = _cast_piece_rows(rows, steps)

        def piece(*g, last=rows // pr - 1):
            return (jnp.minimum(step_index(*g), last), 0)

        arrays.append(w)
        in_specs.append(_stacked_spec(lead, (pr, cols), piece))
        out_specs.append(pl.BlockSpec((pr, cols), piece))
        out_shapes.append(jax.ShapeDtypeStruct((rows, cols), BF16))
    return arrays, in_specs, out_specs, out_shapes


def _with_cast_jobs(body, n_in, n_cast, *refs):
    for src, dst in zip(refs[n_in:n_in + n_cast], refs[n_in + n_cast + 1:n_in + 2 * n_cast + 1]):
        dst[...] = src[...].astype(BF16)
    body(*refs[:n_in], refs[n_in + n_cast], *refs[n_in + 2 * n_cast + 1:])


def _three_stage_call(name, body, x, g_pre, g_post, tm, nj, chunked_inputs, scratch_shapes,
                      cast_next):
    t, d = x.shape
    n_tiles = t // tm
    n_slices = _slices_per_tile(tm, nj)
    sr = tm // n_slices
    cast_arrays, cast_in, cast_out, cast_shapes = _cast_jobs(
        cast_next, (n_tiles + 2) * nj, lambda r, j: r * nj + j)

    def norm_slice(r, j):
        return (jnp.minimum(r, n_tiles - 1) * n_slices + jnp.minimum(j, n_slices - 1), 0)

    def final_slice(r, j):
        return (jnp.where(r < 2, 0, (r - 2) * n_slices + jnp.minimum(j, n_slices - 1)), 0)

    def chunk(r, j):
        return jnp.where((r >= 1) & (r <= n_tiles), j, 0)

    out, *cast = pl.pallas_call(
        functools.partial(_with_cast_jobs, functools.partial(body, n_tiles, n_slices),
                          4 + len(chunked_inputs), len(cast_arrays)),
        grid=(n_tiles + 2, nj),
        in_specs=[
            pl.BlockSpec((sr, d), norm_slice),
            pl.BlockSpec((sr, d), final_slice),
            pl.BlockSpec((1, d), lambda r, j: (0, 0)),
            pl.BlockSpec((1, d), lambda r, j: (0, 0)),
        ] + [spec_fn(chunk) for _, spec_fn in chunked_inputs] + cast_in,
        out_specs=[pl.BlockSpec((sr, d), final_slice)] + cast_out,
        out_shape=[jax.ShapeDtypeStruct((t, d), F32)] + cast_shapes,
        scratch_shapes=[pltpu.VMEM((2, tm, d), BF16), pltpu.VMEM((2, tm, d), F32)] + scratch_shapes,
        compiler_params=_params(("arbitrary", "arbitrary")),
        name=name,
    )(x, x, g_pre, g_post, *[a for a, _ in chunked_inputs], *cast_arrays)
    return out, cast


def _ffn_kernel(n_tiles, n_slices, xn_ref, xf_ref, gpre_ref, gpost_ref, wg_ref, wu_ref, wo_ref,
                o_ref, h_ref, acc_ref):
    def down_lhs(h, anchor, tile):
        groups = []
        for c in range(0, wg_ref.shape[1], MXU_WIDTH):
            g = jnp.dot(h, wg_ref[:, c:c + MXU_WIDTH], preferred_element_type=F32)
            u = jnp.dot(h, wu_ref[:, c:c + MXU_WIDTH], preferred_element_type=F32)
            if c == 0:
                u = anchor(u)
            groups.append((jax.nn.silu(g) * u).astype(BF16))
        return groups

    _three_stage_step(n_tiles, n_slices, 0.5, down_lhs, xn_ref, xf_ref, gpre_ref, gpost_ref,
                      wo_ref, o_ref, h_ref, acc_ref)


def _ffn(x, g_pre, g_post, w_in, w_out, cast_next):
    t, d = x.shape
    f = w_out.shape[-2]
    tm = _pick_tile(t, 1024, BF16_SUBLANES)
    tf = _pick_tile(f, 512, LANES)
    nj = f // tf
    return _three_stage_call(
        "ffn_half_step", _ffn_kernel, x, g_pre, g_post, tm, nj,
        [(w_in, lambda chunk: pl.BlockSpec((d, tf), lambda r, j: (0, chunk(r, j)))),
         (w_in, lambda chunk: pl.BlockSpec((d, tf), lambda r, j: (0, nj + chunk(r, j)))),
         (w_out, lambda chunk: pl.BlockSpec((tf, d), lambda r, j: (chunk(r, j), 0)))],
        [], cast_next)


def _conv_kernel(tiles_per_seq, n_tiles, n_slices, xn_ref, xf_ref, gpre_ref, gpost_ref, wb_ref,
                 wc_ref, wu_ref, cw_ref, wo_ref, o_ref, h_ref, acc_ref, zbuf_ref, halo_ref):
    j = pl.program_id(1)
    tm = h_ref.shape[1]

    @pl.when((pl.program_id(0) == 0) & (j == 0))
    def _():
        halo_ref[...] = jnp.zeros_like(halo_ref)

    def down_lhs(h, anchor, tile):
        groups = []
        for c0 in range(0, wb_ref.shape[1], MXU_WIDTH):
            cols = slice(c0, c0 + MXU_WIDTH)
            b = jnp.dot(h, wb_ref[:, cols], preferred_element_type=F32)
            c = jnp.dot(h, wc_ref[:, cols], preferred_element_type=F32)
            u = jnp.dot(h, wu_ref[:, cols], preferred_element_type=F32)
            if c0 == 0:
                u = anchor(u)
            z = c * u

            zbuf_ref[pl.ds(SUBLANES, tm), cols] = z
            prev = halo_ref[j, :, cols]
            zbuf_ref[pl.ds(0, SUBLANES), cols] = jnp.where(tile % tiles_per_seq == 0,
                                                           jnp.zeros_like(prev), prev)
            halo_ref[j, :, cols] = z[tm - SUBLANES:, :]

            cw = cw_ref[:, cols]
            zc = (cw[0:1, :] * zbuf_ref[pl.ds(SUBLANES - 2, tm), cols]
                  + cw[1:2, :] * zbuf_ref[pl.ds(SUBLANES - 1, tm), cols]
                  + cw[2:3, :] * z)
            groups.append((b * zc).astype(BF16))
        return groups

    _three_stage_step(n_tiles, n_slices, 1.0, down_lhs, xn_ref, xf_ref, gpre_ref, gpost_ref,
                      wo_ref, o_ref, h_ref, acc_ref)


def _conv_mixer(x, seq, g_pre, g_post, w_in, conv_w, w_out, cast_next):
    t, d = x.shape
    tm = _pick_tile(seq, 512, BF16_SUBLANES)
    tn = _pick_tile(d, 512, LANES)
    nj = d // tn

    def w_in_cols(part):
        return lambda chunk: pl.BlockSpec((d, tn), lambda r, j: (0, part * nj + chunk(r, j)))

    return _three_stage_call(
        "short_conv_mixer", functools.partial(_conv_kernel, seq // tm), x, g_pre, g_post, tm, nj,
        [(w_in, w_in_cols(0)), (w_in, w_in_cols(1)), (w_in, w_in_cols(2)),
         (conv_w, lambda chunk: pl.BlockSpec((CONV_WIDTH, tn), lambda r, j: (0, chunk(r, j)))),
         (w_out, lambda chunk: pl.BlockSpec((tn, d), lambda r, j: (chunk(r, j), 0)))],
        [pltpu.VMEM((tm + SUBLANES, tn), F32), pltpu.VMEM((nj, SUBLANES, tn), F32)], cast_next)


def _qkv_pass_col(j):
    return (2 * j) % 3


def _qkv_kernel(x_ref, gpre_ref, w_ref, cos_ref, sin_ref, o_ref, ksum_ref, h_ref):
    j = pl.program_id(1)
    tm, d = x_ref.shape
    half = ROT_DIM // 2
    nb = tm // MOBA_BLOCK

    @pl.when(j == 0)
    def _():
        h_ref[...] = _rms(x_ref[...], gpre_ref[...]).astype(BF16)

    rotary = j != 1
    cos = jnp.where(rotary, cos_ref[...], 1.0)
    sin = jnp.where(rotary, sin_ref[...], 0.0)
    lane = lax.broadcasted_iota(jnp.int32, (tm, HEAD_DIM), 1)
    pad_rows = jnp.zeros((SUBLANES - nb, HEAD_DIM), F32)
    h = h_ref[...]
    for c in range(d // MXU_WIDTH):
        r = jnp.dot(h, w_ref[:, c * MXU_WIDTH:(c + 1) * MXU_WIDTH], preferred_element_type=F32)
        for hh in range(MXU_WIDTH // HEAD_DIM):
            th = r[:, hh * HEAD_DIM:(hh + 1) * HEAD_DIM]
            partner = jnp.where(lane < half,
                                pltpu.roll(th, HEAD_DIM - half, 1),
                                pltpu.roll(th, half, 1))
            rot = th * cos + partner * sin
            cols = slice(c * MXU_WIDTH + hh * HEAD_DIM, c * MXU_WIDTH + (hh + 1) * HEAD_DIM)
            o_ref[:, cols] = rot.astype(BF16)
            sums = jnp.sum(rot.reshape(nb, MOBA_BLOCK, HEAD_DIM), axis=1)
            ksum_ref[0, :, cols] = jnp.concatenate([sums, pad_rows], axis=0) if nb < SUBLANES else sums


def _rope_tables(seq):
    half = ROT_DIM // 2
    inv_freq = ROPE_THETA ** (-jnp.arange(0, ROT_DIM, 2, dtype=F32) / ROT_DIM)
    ang = jnp.arange(seq, dtype=F32)[:, None] * inv_freq[None, :]
    cos, sin = jnp.cos(ang), jnp.sin(ang)
    pad = HEAD_DIM - ROT_DIM
    cos_t = jnp.concatenate([cos, cos, jnp.ones((seq, pad), F32)], axis=-1)
    sin_t = jnp.concatenate([-sin, sin, jnp.zeros((seq, pad), F32)], axis=-1)
    return cos_t, sin_t


def _qkv_proj(x, seq, g_pre, w_qkv):
    t, d = x.shape
    tm = _pick_tile(seq, 1024, MOBA_BLOCK)
    assert tm // MOBA_BLOCK <= SUBLANES
    tiles_per_seq = seq // tm
    cos_t, sin_t = _rope_tables(seq)
    qkv, ksum = pl.pallas_call(
        _qkv_kernel,
        grid=(t // tm, 3),
        in_specs=[
            pl.BlockSpec((tm, d), lambda i, j: (i, 0)),
            pl.BlockSpec((1, d), lambda i, j: (0, 0)),
            pl.BlockSpec((d, d), lambda i, j: (0, _qkv_pass_col(j))),
            pl.BlockSpec((tm, HEAD_DIM), lambda i, j: (i % tiles_per_seq, 0)),
            pl.BlockSpec((tm, HEAD_DIM), lambda i, j: (i % tiles_per_seq, 0)),
        ],
        out_specs=[
            pl.BlockSpec((tm, d), lambda i, j: (i, _qkv_pass_col(j))),
            pl.BlockSpec((1, SUBLANES, d), lambda i, j: (i, 0, 0)),
        ],
        out_shape=[
            jax.ShapeDtypeStruct((t, 3 * d), BF16),
            jax.ShapeDtypeStruct((t // tm, SUBLANES, d), F32),
        ],
        scratch_shapes=[pltpu.VMEM((tm, d), BF16)],
        compiler_params=_params(("arbitrary", "arbitrary")),
        name="qkv_rope_proj",
    )(x, g_pre, w_qkv, cos_t, sin_t)
    ksum = ksum[:, :tm // MOBA_BLOCK, :].reshape(t // MOBA_BLOCK, d)
    return qkv, ksum


MOBA_HEADS_PER_STEP = 2


def _moba_kernel(q_ref, k_ref, v_ref, ksum_ref, o_ref, s_ref):
    heads = []
    for hh in range(q_ref.shape[1] // HEAD_DIM):
        cols = pl.ds(hh * HEAD_DIM, HEAD_DIM)
        heads.append((q_ref.at[:, cols], k_ref.at[:, cols], v_ref.at[:, cols], ksum_ref.at[:, cols],
                      o_ref.at[:, cols], s_ref.at[hh]))
    maxes = [_moba_scores(q, k, ksum, s) for q, k, v, ksum, o, s in heads]
    for (q, k, v, ksum, o, s), m_of in zip(heads, maxes):
        _moba_outputs(v, o, s, m_of)


def _tile_index(i, jb):
    return i * (i + 1) // 2 + jb


def _moba_scores(q_ref, k_ref, ksum_ref, s_ref):
    seq = q_ref.shape[0]
    nb = seq // MOBA_BLOCK
    blk = MOBA_BLOCK
    nt = (((1,), (1,)), ((), ()))

    kmean = (ksum_ref[...] * (1.0 / blk)).astype(BF16)
    gate = lax.dot_general(kmean, q_ref[...], nt, preferred_element_type=F32)
    n_idx = lax.broadcasted_iota(jnp.int32, (nb, seq), 0)
    q_blk = lax.broadcasted_iota(jnp.int32, (nb, seq), 1) // blk
    rank = jnp.zeros((nb, seq), jnp.int32)
    for m in range(nb - 1):
        gm = gate[m:m + 1, :]
        beats = (gm > gate) | ((gm == gate) & (m < n_idx))
        rank = rank + jnp.where(beats & (m < q_blk), 1, 0)
    sel = jnp.where((n_idx < q_blk) & (rank < MOBA_TOPK), 1.0, 0.0).astype(F32)

    key_i = lax.broadcasted_iota(jnp.int32, (blk, blk), 0)
    qry_i = lax.broadcasted_iota(jnp.int32, (blk, blk), 1)
    causal = key_i <= qry_i

    m_of = {}
    for i in reversed(range(nb)):
        qi = q_ref[i * blk:(i + 1) * blk, :]
        m_run = None
        for jb in range(i + 1):
            kj = k_ref[jb * blk:(jb + 1) * blk, :]
            s = lax.dot_general(kj, qi, nt, preferred_element_type=F32)
            if jb == i:
                s = jnp.where(causal, s, NEG)
            elif i > MOBA_TOPK:
                s = jnp.where(sel[jb:jb + 1, i * blk:(i + 1) * blk] > 0.5, s, NEG)
            s_ref[_tile_index(i, jb)] = s
            m_tile = jnp.max(s.reshape(blk // SUBLANES, SUBLANES, blk), axis=0)
            m_run = m_tile if m_run is None else jnp.maximum(m_run, m_tile)
        m_of[i] = jnp.max(m_run, axis=0, keepdims=True)
    return m_of


def _moba_outputs(v_ref, o_ref, s_ref, m_of):
    blk = MOBA_BLOCK
    tn = (((0,), (0,)), ((), ()))
    exp2_scale = HEAD_DIM ** -0.5 * LOG2_E
    for i in sorted(m_of, reverse=True):
        l_run = None
        acc = None
        for jb in range(i + 1):
            p = jnp.exp2((s_ref[_tile_index(i, jb)] - m_of[i]) * exp2_scale)
            l_tile = jnp.sum(p.reshape(blk // SUBLANES, SUBLANES, blk), axis=0)
            l_run = l_tile if l_run is None else l_run + l_tile
            vj = v_ref[jb * blk:(jb + 1) * blk, :]
            pv = lax.dot_general(vj, p.astype(BF16), tn, preferred_element_type=F32)
            acc = pv if acc is None else acc + pv
        l = jnp.sum(l_run, axis=0, keepdims=True)
        o_ref[i * blk:(i + 1) * blk, :] = (acc / l).T.astype(BF16)


def _moba_attention(qkv, ksum, bsz, seq, cast_next):
    t, d3 = qkv.shape
    d = d3 // 3
    hp = MOBA_HEADS_PER_STEP if (d // HEAD_DIM) % MOBA_HEADS_PER_STEP == 0 else 1
    nh = d // (HEAD_DIM * hp)
    nb = seq // MOBA_BLOCK
    width = hp * HEAD_DIM
    cast_arrays, cast_in, cast_out, cast_shapes = _cast_jobs(
        cast_next, bsz * nh, lambda b, h: b * nh + h)
    out, *cast = pl.pallas_call(
        functools.partial(_with_cast_jobs, _moba_kernel, 4, len(cast_arrays)),
        grid=(bsz, nh),
        in_specs=[
            pl.BlockSpec((seq, width), lambda b, h: (b, h)),
            pl.BlockSpec((seq, width), lambda b, h: (b, nh + h)),
            pl.BlockSpec((seq, width), lambda b, h: (b, 2 * nh + h)),
            pl.BlockSpec((nb, width), lambda b, h: (b, h)),
        ] + cast_in,
        out_specs=[pl.BlockSpec((seq, width), lambda b, h: (b, h))] + cast_out,
        out_shape=[jax.ShapeDtypeStruct((t, d), BF16)] + cast_shapes,
        scratch_shapes=[pltpu.VMEM((hp, nb * (nb + 1) // 2, MOBA_BLOCK, MOBA_BLOCK), F32)],
        compiler_params=_params(("arbitrary", "arbitrary")),
        name="moba_attention",
    )(qkv, qkv, qkv, ksum, *cast_arrays)
    return out, cast


def _oproj_kernel(x_ref, o_ref, w_ref, gpost_ref, out_ref):
    m = jnp.dot(o_ref[...], w_ref[...], preferred_element_type=F32)
    out_ref[...] = x_ref[...] + _rms(m, gpost_ref[...])


def _out_proj(x, o, w_o, g_post):
    t, d = x.shape
    tm = _pick_tile(t, 512, SUBLANES)
    return pl.pallas_call(
        _oproj_kernel,
        grid=(t // tm,),
        in_specs=[
            pl.BlockSpec((tm, d), lambda i: (i, 0)),
            pl.BlockSpec((tm, d), lambda i: (i, 0)),
            pl.BlockSpec((d, d), lambda i: (0, 0)),
            pl.BlockSpec((1, d), lambda i: (0, 0)),
        ],
        out_specs=pl.BlockSpec((tm, d), lambda i: (i, 0)),
        out_shape=jax.ShapeDtypeStruct((t, d), F32),
        compiler_params=_params(("arbitrary",)),
        name="attn_out_proj",
    )(x, o, w_o, g_post)


def kernel(x, norm_g, ffn_w_in, ffn_w_out, conv_w_in, conv_w, conv_w_out, attn_w_qkv, attn_w_o):
    bsz, seq, d = x.shape
    depth = norm_g.shape[0]
    assert d % HEAD_DIM == 0 and seq % MOBA_BLOCK == 0
    assert min(MOBA_TOPK, seq // MOBA_BLOCK - 1) == MOBA_TOPK
    h = x.reshape(bsz * seq, d)

    def ffn_weights(i, half):
        return [(ffn_w_in, (i, half)), (ffn_w_out, (i, half))]

    def mixer_weights(i):
        if i % 2 == 0:
            return [(conv_w_in, (i // 2,)), (conv_w_out, (i // 2,))]
        return [(attn_w_qkv, (i // 2,)), (attn_w_o, (i // 2,))]

    w_in, w_out = ffn_w_in[0, 0].astype(BF16), ffn_w_out[0, 0].astype(BF16)
    for i in range(depth):
        g = norm_g[i][:, None, :]
        h, (wm_a, wm_b) = _ffn(h, g[0], g[1], w_in, w_out, mixer_weights(i))
        if i % 2 == 0:
            h, (w_in, w_out) = _conv_mixer(h, seq, g[2], g[3], wm_a, conv_w[i // 2], wm_b,
                                           ffn_weights(i, 1))
        else:
            qkv, ksum = _qkv_proj(h, seq, g[2], wm_a)
            o, (w_in, w_out) = _moba_attention(qkv, ksum, bsz, seq, ffn_weights(i, 1))
            h = _out_proj(h, o, wm_b, g[3])
        last = i == depth - 1
        h, nxt = _ffn(h, g[4], g[5], w_in, w_out, [] if last else ffn_weights(i + 1, 0))
        if not last:
            w_in, w_out = nxt
    return h.reshape(bsz, seq, d)
```

```python
import functools

import jax
import jax.numpy as jnp
from jax import lax
from jax.experimental import pallas as pl
from jax.experimental.pallas import tpu as pltpu

HEAD_DIM = 128
ROT_DIM = HEAD_DIM // 4
ROPE_THETA = 500000.0
MOBA_BLOCK = 256
MOBA_TOPK = 3
CONV_WIDTH = 3
RMS_EPS = 1e-6
NEG = -1e30
LOG2_E = 1.4426950408889634

LANES = 128
SUBLANES = 8
MXU_WIDTH = 256
V7X_VMEM_LIMIT_BYTES = 56 * 1024 * 1024

F32 = jnp.float32
BF16 = jnp.bfloat16


def _pick_tile(n, target, multiple):
    t = min(n, target)
    t -= t % multiple
    while t > multiple and n % t:
        t -= multiple
    assert t > 0 and n % t == 0, (n, target, multiple)
    return t


def _rms(x, g):
    ms = jnp.mean(x * x, axis=-1, keepdims=True)
    return (x * lax.rsqrt(ms + RMS_EPS)) * g


def _params(sem):
    return pltpu.CompilerParams(dimension_semantics=sem,
                                vmem_limit_bytes=V7X_VMEM_LIMIT_BYTES)


def _stacked_spec(lead, block, index_fn):
    lead = tuple(lead)
    return pl.BlockSpec((None,) * len(lead) + tuple(block), lambda *g: lead + tuple(index_fn(*g)))


BF16_SUBLANES = 16


def _slices_per_tile(tm, nj):
    n = min(nj, tm // BF16_SUBLANES)
    while tm % n or (tm // n) % BF16_SUBLANES:
        n -= 1
    return n


def _dependent_zero(*vals):
    words = []
    for v in vals:
        bits = pltpu.bitcast(v, jnp.uint32)
        words += [bits[s:s + SUBLANES, c:c + LANES]
                  for c in range(0, bits.shape[1], LANES) for s in range(0, bits.shape[0], SUBLANES)]
    while len(words) > 1:
        words = [a | b for a, b in zip(words[0::2], words[1::2])] + words[len(words) & ~1:]
    return (words[0] >> 16) >> 16


def _three_stage_step(n_tiles, n_slices, res_scale, down_lhs, xn_ref, xf_ref, gpre_ref, gpost_ref,
                      wo_ref, o_ref, h_ref, acc_ref):
    r = pl.program_id(0)
    j = pl.program_id(1)
    sr = xn_ref.shape[0]
    rows = pl.ds(pl.multiple_of(jnp.minimum(j, n_slices - 1) * sr, sr), sr)
    norm_slot = r % 2
    mm_slot = 1 - norm_slot

    @pl.when((r == 0) & (j == 0))
    def _():
        acc_ref[...] = jnp.zeros_like(acc_ref)

    def finalize_slice():
        out = xf_ref[...] + res_scale * _rms(acc_ref[norm_slot, rows, :], gpost_ref[...])
        o_ref[...] = out
        return out

    def norm_slice():
        hn = _rms(xn_ref[...], gpre_ref[...])
        h_ref[norm_slot, rows, :] = hn.astype(BF16)
        return hn

    def anchor(v):
        zero = _dependent_zero(finalize_slice(), norm_slice())
        zero = jnp.tile(zero, (v.shape[0] // SUBLANES, v.shape[1] // LANES))
        return pltpu.bitcast(pltpu.bitcast(v, jnp.uint32) | zero, F32)

    has_matmul = (r >= 1) & (r <= n_tiles)

    @pl.when(has_matmul)
    def _():
        k0 = 0
        for a in down_lhs(h_ref[mm_slot], anchor, r - 1):
            kc = a.shape[1]
            prev = jnp.where(j == 0, 0.0, acc_ref[mm_slot]) if k0 == 0 else acc_ref[mm_slot]
            acc_ref[mm_slot] = jnp.dot(a, wo_ref[k0:k0 + kc, :], preferred_element_type=F32) + prev
            k0 += kc

    @pl.when(jnp.logical_not(has_matmul))
    def _():
        finalize_slice()
        norm_slice()


def _cast_piece_rows(rows, steps):
    for pr in range(BF16_SUBLANES, rows + 1, BF16_SUBLANES):
        if rows % pr == 0 and rows // pr <= steps:
            return pr
    raise ValueError((rows, steps))


def _cast_jobs(cast_next, steps, step_index):
    arrays, in_specs, out_specs, out_shapes = [], [], [], []
    for w, lead in cast_next:
        rows, cols = w.shape[-2:]
        pr = _cast_piece_rows(rows, steps)

        def piece(*g, last=rows // pr - 1):
            return (jnp.minimum(step_index(*g), last), 0)

        arrays.append(w)
        in_specs.append(_stacked_spec(lead, (pr, cols), piece))
        out_specs.append(pl.BlockSpec((pr, cols), piece))
        out_shapes.append(jax.ShapeDtypeStruct((rows, cols), BF16))
    return arrays, in_specs, out_specs, out_shapes


def _with_cast_jobs(body, n_in, n_cast, *refs):
    for src, dst in zip(refs[n_in:n_in + n_cast], refs[n_in + n_cast + 1:n_in + 2 * n_cast + 1]):
        dst[...] = src[...].astype(BF16)
    body(*refs[:n_in], refs[n_in + n_cast], *refs[n_in + 2 * n_cast + 1:])


def _three_stage_call(name, body, x, g_pre, g_post, tm, nj, chunked_inputs, scratch_shapes,
                      cast_next):
    t, d = x.shape
    n_tiles = t // tm
    n_slices = _slices_per_tile(tm, nj)
    sr = tm // n_slices
    cast_arrays, cast_in, cast_out, cast_shapes = _cast_jobs(
        cast_next, (n_tiles + 2) * nj, lambda r, j: r * nj + j)

    def norm_slice(r, j):
        return (jnp.minimum(r, n_tiles - 1) * n_slices + jnp.minimum(j, n_slices - 1), 0)

    def final_slice(r, j):
        return (jnp.where(r < 2, 0, (r - 2) * n_slices + jnp.minimum(j, n_slices - 1)), 0)

    def chunk(r, j):
        return jnp.where((r >= 1) & (r <= n_tiles), j, 0)

    out, *cast = pl.pallas_call(
        functools.partial(_with_cast_jobs, functools.partial(body, n_tiles, n_slices),
                          4 + len(chunked_inputs), len(cast_arrays)),
        grid=(n_tiles + 2, nj),
        in_specs=[
            pl.BlockSpec((sr, d), norm_slice),
            pl.BlockSpec((sr, d), final_slice),
            pl.BlockSpec((1, d), lambda r, j: (0, 0)),
            pl.BlockSpec((1, d), lambda r, j: (0, 0)),
        ] + [spec_fn(chunk) for _, spec_fn in chunked_inputs] + cast_in,
        out_specs=[pl.BlockSpec((sr, d), final_slice)] + cast_out,
        out_shape=[jax.ShapeDtypeStruct((t, d), F32)] + cast_shapes,
        scratch_shapes=[pltpu.VMEM((2, tm, d), BF16), pltpu.VMEM((2, tm, d), F32)] + scratch_shapes,
        compiler_params=_params(("arbitrary", "arbitrary")),
        name=name,
    )(x, x, g_pre, g_post, *[a for a, _ in chunked_inputs], *cast_arrays)
    return out, cast


def _ffn_kernel(n_tiles, n_slices, xn_ref, xf_ref, gpre_ref, gpost_ref, wg_ref, wu_ref, wo_ref,
                o_ref, h_ref, acc_ref):
    def down_lhs(h, anchor, tile):
        groups = []
        for c in range(0, wg_ref.shape[1], MXU_WIDTH):
            g = jnp.dot(h, wg_ref[:, c:c + MXU_WIDTH], preferred_element_type=F32)
            u = jnp.dot(h, wu_ref[:, c:c + MXU_WIDTH], preferred_element_type=F32)
            if c == 0:
                u = anchor(u)
            groups.append((jax.nn.silu(g) * u).astype(BF16))
        return groups

    _three_stage_step(n_tiles, n_slices, 0.5, down_lhs, xn_ref, xf_ref, gpre_ref, gpost_ref,
                      wo_ref, o_ref, h_ref, acc_ref)


def _ffn(x, g_pre, g_post, w_in, w_out, cast_next):
    t, d = x.shape
    f = w_out.shape[-2]
    tm = _pick_tile(t, 1024, BF16_SUBLANES)
    tf = _pick_tile(f, 512, LANES)
    nj = f // tf
    return _three_stage_call(
        "ffn_half_step", _ffn_kernel, x, g_pre, g_post, tm, nj,
        [(w_in, lambda chunk: pl.BlockSpec((d, tf), lambda r, j: (0, chunk(r, j)))),
         (w_in, lambda chunk: pl.BlockSpec((d, tf), lambda r, j: (0, nj + chunk(r, j)))),
         (w_out, lambda chunk: pl.BlockSpec((tf, d), lambda r, j: (chunk(r, j), 0)))],
        [], cast_next)


def _conv_kernel(tiles_per_seq, n_tiles, n_slices, xn_ref, xf_ref, gpre_ref, gpost_ref, wb_ref,
                 wc_ref, wu_ref, cw_ref, wo_ref, o_ref, h_ref, acc_ref, zbuf_ref, halo_ref):
    j = pl.program_id(1)
    tm = h_ref.shape[1]

    @pl.when((pl.program_id(0) == 0) & (j == 0))
    def _():
        halo_ref[...] = jnp.zeros_like(halo_ref)

    def down_lhs(h, anchor, tile):
        groups = []
        for c0 in range(0, wb_ref.shape[1], MXU_WIDTH):
            cols = slice(c0, c0 + MXU_WIDTH)
            b = jnp.dot(h, wb_ref[:, cols], preferred_element_type=F32)
            c = jnp.dot(h, wc_ref[:, cols], preferred_element_type=F32)
            u = jnp.dot(h, wu_ref[:, cols], preferred_element_type=F32)
            if c0 == 0:
                u = anchor(u)
            z = c * u

            zbuf_ref[pl.ds(SUBLANES, tm), cols] = z
            prev = halo_ref[j, :, cols]
            zbuf_ref[pl.ds(0, SUBLANES), cols] = jnp.where(tile % tiles_per_seq == 0,
                                                           jnp.zeros_like(prev), prev)
            halo_ref[j, :, cols] = z[tm - SUBLANES:, :]

            cw = cw_ref[:, cols]
            zc = (cw[0:1, :] * zbuf_ref[pl.ds(SUBLANES - 2, tm), cols]
                  + cw[1:2, :] * zbuf_ref[pl.ds(SUBLANES - 1, tm), cols]
                  + cw[2:3, :] * z)
            groups.append((b * zc).astype(BF16))
        return groups

    _three_stage_step(n_tiles, n_slices, 1.0, down_lhs, xn_ref, xf_ref, gpre_ref, gpost_ref,
                      wo_ref, o_ref, h_ref, acc_ref)


def _conv_mixer(x, seq, g_pre, g_post, w_in, conv_w, w_out, cast_next):
    t, d = x.shape
    tm = _pick_tile(seq, 512, BF16_SUBLANES)
    tn = _pick_tile(d, 512, LANES)
    nj = d // tn

    def w_in_cols(part):
        return lambda chunk: pl.BlockSpec((d, tn), lambda r, j: (0, part * nj + chunk(r, j)))

    return _three_stage_call(
        "short_conv_mixer", functools.partial(_conv_kernel, seq // tm), x, g_pre, g_post, tm, nj,
        [(w_in, w_in_cols(0)), (w_in, w_in_cols(1)), (w_in, w_in_cols(2)),
         (conv_w, lambda chunk: pl.BlockSpec((CONV_WIDTH, tn), lambda r, j: (0, chunk(r, j)))),
         (w_out, lambda chunk: pl.BlockSpec((tn, d), lambda r, j: (chunk(r, j), 0)))],
        [pltpu.VMEM((tm + SUBLANES, tn), F32), pltpu.VMEM((nj, SUBLANES, tn), F32)], cast_next)


def _qkv_pass_col(j):
    return (2 * j) % 3


def _qkv_kernel(x_ref, gpre_ref, w_ref, cos_ref, sin_ref, o_ref, ksum_ref, h_ref):
    j = pl.program_id(1)
    tm, d = x_ref.shape
    half = ROT_DIM // 2
    nb = tm // MOBA_BLOCK

    @pl.when(j == 0)
    def _():
        h_ref[...] = _rms(x_ref[...], gpre_ref[...]).astype(BF16)

    rotary = j != 1
    cos = jnp.where(rotary, cos_ref[...], 1.0)
    sin = jnp.where(rotary, sin_ref[...], 0.0)
    lane = lax.broadcasted_iota(jnp.int32, (tm, HEAD_DIM), 1)
    pad_rows = jnp.zeros((SUBLANES - nb, HEAD_DIM), F32)
    h = h_ref[...]
    for c in range(d // MXU_WIDTH):
        r = jnp.dot(h, w_ref[:, c * MXU_WIDTH:(c + 1) * MXU_WIDTH], preferred_element_type=F32)
        for hh in range(MXU_WIDTH // HEAD_DIM):
            th = r[:, hh * HEAD_DIM:(hh + 1) * HEAD_DIM]
            partner = jnp.where(lane < half,
                                pltpu.roll(th, HEAD_DIM - half, 1),
                                pltpu.roll(th, half, 1))
            rot = th * cos + partner * sin
            cols = slice(c * MXU_WIDTH + hh * HEAD_DIM, c * MXU_WIDTH + (hh + 1) * HEAD_DIM)
            o_ref[:, cols] = rot.astype(BF16)
            sums = jnp.sum(rot.reshape(nb, MOBA_BLOCK, HEAD_DIM), axis=1)
            ksum_ref[0, :, cols] = jnp.concatenate([sums, pad_rows], axis=0) if nb < SUBLANES else sums


def _rope_tables(seq):
    half = ROT_DIM // 2
    inv_freq = ROPE_THETA ** (-jnp.arange(0, ROT_DIM, 2, dtype=F32) / ROT_DIM)
    ang = jnp.arange(seq, dtype=F32)[:, None] * inv_freq[None, :]
    cos, sin = jnp.cos(ang), jnp.sin(ang)
    pad = HEAD_DIM - ROT_DIM
    cos_t = jnp.concatenate([cos, cos, jnp.ones((seq, pad), F32)], axis=-1)
    sin_t = jnp.concatenate([-sin, sin, jnp.zeros((seq, pad), F32)], axis=-1)
    return cos_t, sin_t


def _qkv_proj(x, seq, g_pre, w_qkv):
    t, d = x.shape
    tm = _pick_tile(seq, 1024, MOBA_BLOCK)
    assert tm // MOBA_BLOCK <= SUBLANES
    tiles_per_seq = seq // tm
    cos_t, sin_t = _rope_tables(seq)
    qkv, ksum = pl.pallas_call(
        _qkv_kernel,
        grid=(t // tm, 3),
        in_specs=[
            pl.BlockSpec((tm, d), lambda i, j: (i, 0)),
            pl.BlockSpec((1, d), lambda i, j: (0, 0)),
            pl.BlockSpec((d, d), lambda i, j: (0, _qkv_pass_col(j))),
            pl.BlockSpec((tm, HEAD_DIM), lambda i, j: (i % tiles_per_seq, 0)),
            pl.BlockSpec((tm, HEAD_DIM), lambda i, j: (i % tiles_per_seq, 0)),
        ],
        out_specs=[
            pl.BlockSpec((tm, d), lambda i, j: (i, _qkv_pass_col(j))),
            pl.BlockSpec((1, SUBLANES, d), lambda i, j: (i, 0, 0)),
        ],
        out_shape=[
            jax.ShapeDtypeStruct((t, 3 * d), BF16),
            jax.ShapeDtypeStruct((t // tm, SUBLANES, d), F32),
        ],
        scratch_shapes=[pltpu.VMEM((tm, d), BF16)],
        compiler_params=_params(("arbitrary", "arbitrary")),
        name="qkv_rope_proj",
    )(x, g_pre, w_qkv, cos_t, sin_t)
    ksum = ksum[:, :tm // MOBA_BLOCK, :].reshape(t // MOBA_BLOCK, d)
    return qkv, ksum


MOBA_HEADS_PER_STEP = 4
MOBA_HEADS_RESIDENT = 2


def _moba_kernel(q_ref, k_ref, v_ref, ksum_ref, o_ref, s_ref):
    n_buf = s_ref.shape[0]
    heads = []
    for hh in range(q_ref.shape[1] // HEAD_DIM):
        cols = pl.ds(hh * HEAD_DIM, HEAD_DIM)
        heads.append((q_ref.at[:, cols], k_ref.at[:, cols], v_ref.at[:, cols], ksum_ref.at[:, cols],
                      o_ref.at[:, cols], s_ref.at[hh % n_buf]))
    for g0 in range(0, len(heads), n_buf):
        group = heads[g0:g0 + n_buf]
        maxes = [_moba_scores(q, k, ksum, s) for q, k, v, ksum, o, s in group]
        for (q, k, v, ksum, o, s), m_of in zip(group, maxes):
            _moba_outputs(v, o, s, m_of)


def _tile_index(i, jb):
    return i * (i + 1) // 2 + jb


def _moba_scores(q_ref, k_ref, ksum_ref, s_ref):
    seq = q_ref.shape[0]
    nb = seq // MOBA_BLOCK
    blk = MOBA_BLOCK
    nt = (((1,), (1,)), ((), ()))

    kmean = (ksum_ref[...] * (1.0 / blk)).astype(BF16)
    gate = lax.dot_general(kmean, q_ref[...], nt, preferred_element_type=F32)
    n_idx = lax.broadcasted_iota(jnp.int32, (nb, seq), 0)
    q_blk = lax.broadcasted_iota(jnp.int32, (nb, seq), 1) // blk
    rank = jnp.zeros((nb, seq), jnp.int32)
    for m in range(nb - 1):
        gm = gate[m:m + 1, :]
        beats = (gm > gate) | ((gm == gate) & (m < n_idx))
        rank = rank + jnp.where(beats & (m < q_blk), 1, 0)
    sel = jnp.where((n_idx < q_blk) & (rank < MOBA_TOPK), 1.0, 0.0).astype(F32)

    key_i = lax.broadcasted_iota(jnp.int32, (blk, blk), 0)
    qry_i = lax.broadcasted_iota(jnp.int32, (blk, blk), 1)
    causal = key_i <= qry_i

    m_of = {}
    for i in reversed(range(nb)):
        qi = q_ref[i * blk:(i + 1) * blk, :]
        m_run = None
        for jb in range(i + 1):
            kj = k_ref[jb * blk:(jb + 1) * blk, :]
            s = lax.dot_general(kj, qi, nt, preferred_element_type=F32)
            if jb == i:
                s = jnp.where(causal, s, NEG)
            elif i > MOBA_TOPK:
                s = jnp.where(sel[jb:jb + 1, i * blk:(i + 1) * blk] > 0.5, s, NEG)
            s_ref[_tile_index(i, jb)] = s
            m_tile = jnp.max(s.reshape(blk // SUBLANES, SUBLANES, blk), axis=0)
            m_run = m_tile if m_run is None else jnp.maximum(m_run, m_tile)
        m_of[i] = jnp.max(m_run, axis=0, keepdims=True)
    return m_of


def _moba_outputs(v_ref, o_ref, s_ref, m_of):
    blk = MOBA_BLOCK
    tn = (((0,), (0,)), ((), ()))
    exp2_scale = HEAD_DIM ** -0.5 * LOG2_E
    for i in sorted(m_of, reverse=True):
        l_run = None
        acc = None
        for jb in range(i + 1):
            p = jnp.exp2((s_ref[_tile_index(i, jb)] - m_of[i]) * exp2_scale)
            l_tile = jnp.sum(p.reshape(blk // SUBLANES, SUBLANES, blk), axis=0)
            l_run = l_tile if l_run is None else l_run + l_tile
            vj = v_ref[jb * blk:(jb + 1) * blk, :]
            pv = lax.dot_general(vj, p.astype(BF16), tn, preferred_element_type=F32)
            acc = pv if acc is None else acc + pv
        l = jnp.sum(l_run, axis=0, keepdims=True)
        o_ref[i * blk:(i + 1) * blk, :] = (acc / l).T.astype(BF16)


def _moba_attention(qkv, ksum, bsz, seq, cast_next):
    t, d3 = qkv.shape
    d = d3 // 3
    hp = MOBA_HEADS_PER_STEP if (d // HEAD_DIM) % MOBA_HEADS_PER_STEP == 0 else 1
    nh = d // (HEAD_DIM * hp)
    nb = seq // MOBA_BLOCK
    width = hp * HEAD_DIM
    cast_arrays, cast_in, cast_out, cast_shapes = _cast_jobs(
        cast_next, bsz * nh, lambda b, h: b * nh + h)
    out, *cast = pl.pallas_call(
        functools.partial(_with_cast_jobs, _moba_kernel, 4, len(cast_arrays)),
        grid=(bsz, nh),
        in_specs=[
            pl.BlockSpec((seq, width), lambda b, h: (b, h)),
            pl.BlockSpec((seq, width), lambda b, h: (b, nh + h)),
            pl.BlockSpec((seq, width), lambda b, h: (b, 2 * nh + h)),
            pl.BlockSpec((nb, width), lambda b, h: (b, h)),
        ] + cast_in,
        out_specs=[pl.BlockSpec((seq, width), lambda b, h: (b, h))] + cast_out,
        out_shape=[jax.ShapeDtypeStruct((t, d), BF16)] + cast_shapes,
        scratch_shapes=[pltpu.VMEM((min(hp, MOBA_HEADS_RESIDENT), nb * (nb + 1) // 2, MOBA_BLOCK,
                                    MOBA_BLOCK), F32)],
        compiler_params=_params(("arbitrary", "arbitrary")),
        name="moba_attention",
    )(qkv, qkv, qkv, ksum, *cast_arrays)
    return out, cast


def _oproj_kernel(x_ref, o_ref, w_ref, gpost_ref, out_ref):
    m = jnp.dot(o_ref[...], w_ref[...], preferred_element_type=F32)
    out_ref[...] = x_ref[...] + _rms(m, gpost_ref[...])


def _out_proj(x, o, w_o, g_post):
    t, d = x.shape
    tm = _pick_tile(t, 512, SUBLANES)
    return pl.pallas_call(
        _oproj_kernel,
        grid=(t // tm,),
        in_specs=[
            pl.BlockSpec((tm, d), lambda i: (i, 0)),
            pl.BlockSpec((tm, d), lambda i: (i, 0)),
            pl.BlockSpec((d, d), lambda i: (0, 0)),
            pl.BlockSpec((1, d), lambda i: (0, 0)),
        ],
        out_specs=pl.BlockSpec((tm, d), lambda i: (i, 0)),
        out_shape=jax.ShapeDtypeStruct((t, d), F32),
        compiler_params=_params(("arbitrary",)),
        name="attn_out_proj",
    )(x, o, w_o, g_post)


def kernel(x, norm_g, ffn_w_in, ffn_w_out, conv_w_in, conv_w, conv_w_out, attn_w_qkv, attn_w_o):
    bsz, seq, d = x.shape
    depth = norm_g.shape[0]
    assert d % HEAD_DIM == 0 and seq % MOBA_BLOCK == 0
    assert min(MOBA_TOPK, seq // MOBA_BLOCK - 1) == MOBA_TOPK
    h = x.reshape(bsz * seq, d)

    def ffn_weights(i, half):
        return [(ffn_w_in, (i, half)), (ffn_w_out, (i, half))]

    def mixer_weights(i):
        if i % 2 == 0:
            return [(conv_w_in, (i // 2,)), (conv_w_out, (i // 2,))]
        return [(attn_w_qkv, (i // 2,)), (attn_w_o, (i // 2,))]

    w_in, w_out = ffn_w_in[0, 0].astype(BF16), ffn_w_out[0, 0].astype(BF16)
    for i in range(depth):
        g = norm_g[i][:, None, :]
        h, (wm_a, wm_b) = _ffn(h, g[0], g[1], w_in, w_out, mixer_weights(i))
        if i % 2 == 0:
            h, (w_in, w_out) = _conv_mixer(h, seq, g[2], g[3], wm_a, conv_w[i // 2], wm_b,
                                           ffn_weights(i, 1))
        else:
            qkv, ksum = _qkv_proj(h, seq, g[2], wm_a)
            o, (w_in, w_out) = _moba_attention(qkv, ksum, bsz, seq, ffn_weights(i, 1))
            h = _out_proj(h, o, wm_b, g[3])
        last = i == depth - 1
        h, nxt = _ffn(h, g[4], g[5], w_in, w_out, [] if last else ffn_weights(i + 1, 0))
        if not last:
            w_in, w_out = nxt
    return h.reshape(bsz, seq, d)
```
